```python
import math
import jax, jax.numpy as jnp
from jax import lax
import numpy as np

D_MODEL = 1024
BATCH = 4
SEQ = 4096
DEPTH = 4

N_MIXERS = 3
PLE_DIM = 256
EPS = 1e-6
ATTN_HEADS = 16
ATTN_KV_HEADS = 4
ATTN_HEAD_DIM = 64
WINDOW = 128
ATTN_BLOCK = 128
CONV_WIDTH = 31
SSM_EXPAND = 2
SSM_D_INNER = SSM_EXPAND * D_MODEL
SSM_HEAD_DIM = 64
SSM_HEADS = SSM_D_INNER // SSM_HEAD_DIM
SSM_GROUPS = 8
SSM_STATE = 128
SSM_CONV = 4
SSM_CHUNK = 128
MLP_HIDDEN = 4 * D_MODEL

kernel_name = "hybrid_swa_conformer_ssd_trunk"


def rmsnorm(x, g):
    xf = x.astype(jnp.float32)
    y = xf * lax.rsqrt(jnp.mean(xf * xf, axis=-1, keepdims=True) + EPS)
    return (y * g.astype(jnp.float32)).astype(x.dtype)


def layernorm(x, g, b):
    xf = x.astype(jnp.float32)
    mu = jnp.mean(xf, axis=-1, keepdims=True)
    var = jnp.mean(jnp.square(xf - mu), axis=-1, keepdims=True)
    y = (xf - mu) * lax.rsqrt(var + EPS)
    return (y * g.astype(jnp.float32) + b.astype(jnp.float32)).astype(x.dtype)


def alibi_slopes(n_heads):
    return jnp.exp2(-8.0 * (jnp.arange(n_heads, dtype=jnp.float32) + 1.0) / n_heads)


def causal_dwconv(u, w, b):
    k, c = w.shape
    y = lax.conv_general_dilated(
        u, w[:, None, :].astype(u.dtype), window_strides=(1,), padding=[(k - 1, 0)],
        dimension_numbers=('NWC', 'WIO', 'NWC'), feature_group_count=c)
    return y + b


def sliding_window_attention(u, wqkv, gq, gk, sinks, wo):
    b, s, _ = u.shape
    H, KV, HD, BLK = ATTN_HEADS, ATTN_KV_HEADS, ATTN_HEAD_DIM, ATTN_BLOCK
    G = H // KV
    nb = s // BLK
    qkv = u @ wqkv
    q = qkv[..., :H * HD].reshape(b, s, KV, G, HD)
    k = qkv[..., H * HD:(H + KV) * HD].reshape(b, s, KV, HD)
    v = qkv[..., (H + KV) * HD:].reshape(b, s, KV, HD)
    q = rmsnorm(q, gq)
    k = rmsnorm(k, gk)
    qb = q.reshape(b, nb, BLK, KV, G, HD)

    def band(t):
        tb = t.reshape(b, nb, BLK, KV, HD)
        prev = jnp.pad(tb, ((0, 0), (1, 0), (0, 0), (0, 0), (0, 0)))[:, :-1]
        return jnp.concatenate([prev, tb], axis=2)

    kb, vb = band(k), band(v)
    scores = jnp.einsum('bnqkgd,bnskd->bnkgqs', qb, kb).astype(jnp.float32) * (HD ** -0.5)
    qi = jnp.arange(BLK)
    kj = jnp.arange(2 * BLK)
    dist = qi[:, None] - kj[None, :] + BLK
    kpos = jnp.arange(nb)[:, None] * BLK - BLK + kj[None, :]
    mask = ((dist >= 0) & (dist < WINDOW))[None] & (kpos >= 0)[:, None, :]
    slopes = alibi_slopes(H).reshape(KV, G)
    scores = scores - slopes[:, :, None, None] * dist.astype(jnp.float32)
    scores = jnp.where(mask[None, :, None, None], scores, -jnp.inf)
    sink = sinks.astype(jnp.float32).reshape(KV, G)[:, :, None]
    m = jnp.maximum(scores.max(axis=-1), sink)
    e = jnp.exp(scores - m[..., None])
    probs = e / (e.sum(axis=-1) + jnp.exp(sink - m))[..., None]
    o = jnp.einsum('bnkgqs,bnskd->bnqkgd', probs.astype(v.dtype), vb)
    return o.reshape(b, s, H * HD) @ wo


def conformer_conv_module(u, w_pw1, b_pw1, w_dw, b_dw, ln_g, ln_b, w_pw2, b_pw2):
    a = u @ w_pw1 + b_pw1
    val, gate = jnp.split(a, 2, axis=-1)
    y = val * jax.nn.sigmoid(gate)
    y = causal_dwconv(y, w_dw, b_dw)
    y = layernorm(y, ln_g, ln_b)
    y = jax.nn.silu(y)
    return y @ w_pw2 + b_pw2


def ssd_chunked(xs, dt, A, Bm, Cm):
    b, s, _ = xs.shape
    G, N, P, L = SSM_GROUPS, SSM_STATE, SSM_HEAD_DIM, SSM_CHUNK
    R = SSM_HEADS // G
    nc = s // L
    x = xs.reshape(b, nc, L, G, R, P)
    dtc = dt.reshape(b, nc, L, G, R)
    Bc = Bm.reshape(b, nc, L, G, N)
    Cc = Cm.reshape(b, nc, L, G, N)
    dA = dtc * A.reshape(G, R)
    cs = jnp.cumsum(dA, axis=2)
    x_dt = x * dtc[..., None].astype(x.dtype)
    causal = jnp.arange(L)[:, None] >= jnp.arange(L)[None, :]
    diff = cs[:, :, :, None] - cs[:, :, None, :]
    Lmat = jnp.exp(jnp.where(causal[:, :, None, None], diff, -jnp.inf))
    CB = jnp.einsum('bclgn,bcsgn->bclsg', Cc, Bc)
    y_diag = jnp.einsum('bclsg,bclsgr,bcsgrp->bclgrp', CB, Lmat, x_dt)
    decay_states = jnp.exp(cs[:, :, -1:] - cs)
    states = jnp.einsum('bcsgn,bcsgr,bcsgrp->bcgrpn', Bc, decay_states, x_dt).astype(jnp.float32)
    chunk_decay = jnp.exp(cs[:, :, -1])

    def step(h, inp):
        dec, st = inp
        return dec[..., None, None] * h + st, h

    h0 = jnp.zeros((b, G, R, P, N), jnp.float32)
    _, prev = lax.scan(step, h0, (jnp.moveaxis(chunk_decay, 1, 0), jnp.moveaxis(states, 1, 0)))
    prev = jnp.moveaxis(prev, 0, 1)
    y_off = jnp.einsum('bclgn,bcgrpn,bclgr->bclgrp', Cc, prev, jnp.exp(cs))
    return (y_diag + y_off).reshape(b, s, SSM_HEADS * P).astype(xs.dtype)


def mamba2_mixer(u, w_in, w_conv, b_conv, dt_bias, A_log, Dskip, norm_g, w_out):
    b, s, _ = u.shape
    DI, GN = SSM_D_INNER, SSM_GROUPS * SSM_STATE
    zxbcdt = u @ w_in
    z = zxbcdt[..., :DI]
    xbc = zxbcdt[..., DI:2 * DI + 2 * GN]
    dt_raw = zxbcdt[..., 2 * DI + 2 * GN:]
    xbc = jax.nn.silu(causal_dwconv(xbc, w_conv, b_conv))
    xs = xbc[..., :DI]
    Bm = xbc[..., DI:DI + GN]
    Cm = xbc[..., DI + GN:]
    dt = jax.nn.softplus(dt_raw.astype(jnp.float32) + dt_bias.astype(jnp.float32))
    A = -jnp.exp(A_log.astype(jnp.float32))
    y = ssd_chunked(xs, dt, A, Bm, Cm)
    y = y + (xs.reshape(b, s, SSM_HEADS, SSM_HEAD_DIM) * Dskip[:, None]).reshape(b, s, DI)
    y = rmsnorm(y * jax.nn.silu(z), norm_g)
    return y @ w_out


def sqrelu_mlp(u, w1, w2):
    return jnp.square(jax.nn.relu(u @ w1)) @ w2


def setup_inputs(seed: int = 0) -> dict:
    key = jax.random.key(seed)
    ks = jax.random.split(key, 48)
    ctr = [0]

    def nk():
        ctr[0] += 1
        return ks[ctr[0] - 1]

    def w(shape, fan_in):
        return jax.random.normal(nk(), shape, jnp.float32) * fan_in ** -0.5

    def gain(shape):
        return 1.0 + 0.05 * jax.random.normal(nk(), shape, jnp.float32)

    def bias(shape):
        return 0.02 * jax.random.normal(nk(), shape, jnp.float32)

    n_a = len(range(0, DEPTH, N_MIXERS))
    n_b = len(range(1, DEPTH, N_MIXERS))
    n_c = len(range(2, DEPTH, N_MIXERS))
    D = D_MODEL
    qkv_w = (ATTN_HEADS + 2 * ATTN_KV_HEADS) * ATTN_HEAD_DIM
    in_w = 2 * SSM_D_INNER + 2 * SSM_GROUPS * SSM_STATE + SSM_HEADS
    conv_c = SSM_D_INNER + 2 * SSM_GROUPS * SSM_STATE

    x = jax.random.normal(nk(), (BATCH, SEQ, D), jnp.float32)
    p = jax.random.normal(nk(), (DEPTH, BATCH, SEQ, PLE_DIM), jnp.float32)
    dt0 = jnp.exp(jax.random.uniform(nk(), (n_c, SSM_HEADS), jnp.float32,
                                     minval=math.log(1e-3), maxval=math.log(1e-1)))
    c_dt_bias = dt0 + jnp.log(-jnp.expm1(-dt0))
    c_A_log = jnp.log(jax.random.uniform(nk(), (n_c, SSM_HEADS), jnp.float32, minval=1.0, maxval=16.0))
    return {
        "x": x,
        "p": p,
        "mix_norm_g": gain((DEPTH, D)),
        "mlp_norm_g": gain((DEPTH, D)),
        "ple_norm_g": gain((DEPTH, D)),
        "a_wqkv": w((n_a, D, qkv_w), D),
        "a_q_norm_g": gain((n_a, ATTN_HEAD_DIM)),
        "a_k_norm_g": gain((n_a, ATTN_HEAD_DIM)),
        "a_sinks": 0.5 * jax.random.normal(nk(), (n_a, ATTN_HEADS), jnp.float32),
        "a_wo": w((n_a, ATTN_HEADS * ATTN_HEAD_DIM, D), ATTN_HEADS * ATTN_HEAD_DIM),
        "b_w_pw1": w((n_b, D, 2 * D), D),
        "b_b_pw1": bias((n_b, 2 * D)),
        "b_w_dw": w((n_b, CONV_WIDTH, D), CONV_WIDTH),
        "b_b_dw": bias((n_b, D)),
        "b_ln_g": gain((n_b, D)),
        "b_ln_b": bias((n_b, D)),
        "b_w_pw2": w((n_b, D, D), D),
        "b_b_pw2": bias((n_b, D)),
        "c_w_in": w((n_c, D, in_w), D),
        "c_w_conv": w((n_c, SSM_CONV, conv_c), SSM_CONV),
        "c_b_conv": bias((n_c, conv_c)),
        "c_dt_bias": c_dt_bias,
        "c_A_log": c_A_log,
        "c_D": gain((n_c, SSM_HEADS)),
        "c_norm_g": gain((n_c, SSM_D_INNER)),
        "c_w_out": w((n_c, SSM_D_INNER, D), SSM_D_INNER),
        "m_w1": w((DEPTH, D, MLP_HIDDEN), D),
        "m_w2": w((DEPTH, MLP_HIDDEN, D), MLP_HIDDEN),
        "ple_w_proj": w((DEPTH, PLE_DIM, D), PLE_DIM),
        "ple_w_gate": w((DEPTH, D, D), D),
    }


def reference(x, p, mix_norm_g, mlp_norm_g, ple_norm_g,
              a_wqkv, a_q_norm_g, a_k_norm_g, a_sinks, a_wo,
              b_w_pw1, b_b_pw1, b_w_dw, b_b_dw, b_ln_g, b_ln_b, b_w_pw2, b_b_pw2,
              c_w_in, c_w_conv, c_b_conv, c_dt_bias, c_A_log, c_D, c_norm_g, c_w_out,
              m_w1, m_w2, ple_w_proj, ple_w_gate):
    h = x
    for i in range(DEPTH):
        kind = i % N_MIXERS
        j = i // N_MIXERS
        u = rmsnorm(h, mix_norm_g[i])
        if kind == 0:
            mix = sliding_window_attention(u, a_wqkv[j], a_q_norm_g[j], a_k_norm_g[j],
                                           a_sinks[j], a_wo[j])
        elif kind == 1:
            mix = conformer_conv_module(u, b_w_pw1[j], b_b_pw1[j], b_w_dw[j], b_b_dw[j],
                                        b_ln_g[j], b_ln_b[j], b_w_pw2[j], b_b_pw2[j])
        else:
            mix = mamba2_mixer(u, c_w_in[j], c_w_conv[j], c_b_conv[j], c_dt_bias[j],
                               c_A_log[j], c_D[j], c_norm_g[j], c_w_out[j])
        h = h + mix
        h = h + sqrelu_mlp(rmsnorm(h, mlp_norm_g[i]), m_w1[i], m_w2[i])
        gate = jax.nn.sigmoid(rmsnorm(h, ple_norm_g[i]) @ ple_w_gate[i])
        h = h + gate * (p[i] @ ple_w_proj[i])
    return h
```

```python
import functools

import jax
import jax.numpy as jnp
from jax import lax
from jax.experimental import pallas as pl
from jax.experimental.pallas import tpu as pltpu

F32 = jnp.float32
BF16 = jnp.bfloat16
EPS = 1e-6

N_MIXERS = 3
ATTN_HEADS = 16
ATTN_KV_HEADS = 4
ATTN_HEAD_DIM = 64
ATTN_BLOCK = 128
CONV_WIDTH = 31
SSM_HEAD_DIM = 64
SSM_GROUPS = 8
SSM_STATE = 128
SSM_CONV = 4
SSM_CHUNK = 128

LANES = 128
SUBLANES = 8
VMEM_LIMIT_BYTES = 56 * 1024 * 1024


def _params(n_grid_dims):
    return pltpu.CompilerParams(
        dimension_semantics=("arbitrary",) * n_grid_dims,
        vmem_limit_bytes=VMEM_LIMIT_BYTES)


def _const_spec(shape):
    nd = len(shape)
    return pl.BlockSpec(shape, lambda *_: (0,) * nd, pipeline_mode=pl.Buffered(1))


def _rms(x, g):
    return x * lax.rsqrt(jnp.mean(x * x, axis=-1, keepdims=True) + EPS) * g


def _dot(a, b):
    return jnp.dot(a, b, preferred_element_type=F32)


def _dot_nt(a, b):
    return lax.dot_general(a, b, (((1,), (1,)), ((), ())), preferred_element_type=F32)


def _mlp_ple_kernel(h_ref, p_ref, g1_ref, g2_ref, w1_ref, w2_ref, wg_ref, wp_ref, o_ref, *,
                    hidden_chunk):
    x = h_ref[...]
    u = _rms(x, g1_ref[...]).astype(BF16)
    acc = jnp.zeros_like(x)
    for c in range(0, w1_ref.shape[1], hidden_chunk):
        a = _dot(u, w1_ref[:, c:c + hidden_chunk])
        a = jnp.square(jnp.maximum(a, 0.0)).astype(BF16)
        acc = acc + _dot(a, w2_ref[c:c + hidden_chunk, :])
    h2 = x + acc
    u2 = _rms(h2, g2_ref[...]).astype(BF16)
    gate = jax.nn.sigmoid(_dot(u2, wg_ref[...]))
    proj = _dot(p_ref[...].astype(BF16), wp_ref[...])
    o_ref[...] = h2 + gate * proj


def _mlp_ple(h, p, g1, g2, w1, w2, wg, wp, *, tm=512, hidden_chunk=1024):
    t, d = h.shape
    tm = min(tm, t)
    row = lambda i: (i, 0)
    return pl.pallas_call(
        functools.partial(_mlp_ple_kernel, hidden_chunk=hidden_chunk),
        grid=(t // tm,),
        in_specs=[pl.BlockSpec((tm, d), row), pl.BlockSpec((tm, p.shape[1]), row),
                  _const_spec(g1.shape), _const_spec(g2.shape), _const_spec(w1.shape),
                  _const_spec(w2.shape), _const_spec(wg.shape), _const_spec(wp.shape)],
        out_specs=pl.BlockSpec((tm, d), row),
        out_shape=jax.ShapeDtypeStruct((t, d), F32),
        compiler_params=_params(1),
        name="mlp_ple",
    )(h, p, g1, g2, w1, w2, wg, wp)


def _qkv_kernel(h_ref, g_ref, w_ref, gq_ref, gk_ref, q_ref, k_ref, v_ref):
    hd = ATTN_HEAD_DIM
    nq = q_ref.shape[1]
    nk = k_ref.shape[1]
    u = _rms(h_ref[...], g_ref[...]).astype(BF16)
    qkv = _dot(u, w_ref[...])
    left = lax.broadcasted_iota(jnp.int32, (1, LANES), 1) < hd

    def pair_norm(t, g):
        sq = t * t
        ml = jnp.sum(jnp.where(left, sq, 0.0), axis=-1, keepdims=True) * (1.0 / hd)
        mr = jnp.sum(jnp.where(left, 0.0, sq), axis=-1, keepdims=True) * (1.0 / hd)
        inv = jnp.where(left, lax.rsqrt(ml + EPS), lax.rsqrt(mr + EPS))
        return t * inv * g

    for c in range(0, nq, LANES):
        q_ref[:, c:c + LANES] = pair_norm(qkv[:, c:c + LANES], gq_ref[...]).astype(BF16)
    for c in range(0, nk, LANES):
        k_ref[:, c:c + LANES] = pair_norm(qkv[:, nq + c:nq + c + LANES], gk_ref[...]).astype(BF16)
    v_ref[...] = qkv[:, nq + nk:].astype(BF16)


def _attn_kernel(sink_ref, h_ref, q_ref, kc_ref, kp_ref, vc_ref, vp_ref, wo_ref, o_ref, oacc_ref):
    hd, blk = ATTN_HEAD_DIM, ATTN_BLOCK
    group = ATTN_HEADS // ATTN_KV_HEADS
    i = pl.program_id(1)
    tq = q_ref.shape[0]
    row = lax.broadcasted_iota(jnp.int32, (blk, 2 * blk), 0)
    col = lax.broadcasted_iota(jnp.int32, (blk, 2 * blk), 1)
    dist = row - col + blk
    in_window = (dist >= 0) & (dist < blk)
    distf = dist.astype(F32)
    for b0 in range(0, tq, blk):
        if b0 == 0:
            kb = jnp.concatenate([kp_ref[...], kc_ref[0:blk, :]], axis=0)
            vb = jnp.concatenate([vp_ref[...], vc_ref[0:blk, :]], axis=0)
            mask = in_window & (col >= jnp.where(i > 0, 0, blk))
        else:
            kb = kc_ref[b0 - blk:b0 + blk, :]
            vb = vc_ref[b0 - blk:b0 + blk, :]
            mask = in_window
        qb = q_ref[b0:b0 + blk, :]
        for kv in range(ATTN_KV_HEADS):
            kh = kb[:, kv * hd:(kv + 1) * hd]
            vh = vb[:, kv * hd:(kv + 1) * hd]
            for g in range(group):
                hh = kv * group + g
                slope = 2.0 ** (-8.0 * (hh + 1) / ATTN_HEADS)
                s = _dot_nt(qb[:, hh * hd:(hh + 1) * hd], kh)
                s = jnp.where(mask, s - slope * distf, -jnp.inf)
                sink = sink_ref[hh]
                m = jnp.maximum(jnp.max(s, axis=-1, keepdims=True), sink)
                e = jnp.exp(s - m)
                den = jnp.sum(e, axis=-1, keepdims=True) + jnp.exp(sink - m)
                pr = (e / den).astype(BF16)
                oacc_ref[b0:b0 + blk, hh * hd:(hh + 1) * hd] = _dot(pr, vh)
    o_ref[...] = h_ref[...] + _dot(oacc_ref[...].astype(BF16), wo_ref[...])


def _attention_layer(h, batch, seq, g, wqkv, gq, gk, sinks, wo, *, tm=512):
    t, d = h.shape
    hd, blk = ATTN_HEAD_DIM, ATTN_BLOCK
    nq, nk = ATTN_HEADS * hd, ATTN_KV_HEADS * hd
    tm = min(tm, seq)
    row = lambda i: (i, 0)
    q, k, v = pl.pallas_call(
        _qkv_kernel,
        grid=(t // tm,),
        in_specs=[pl.BlockSpec((tm, d), row), _const_spec(g.shape), _const_spec(wqkv.shape),
                  _const_spec(gq.shape), _const_spec(gk.shape)],
        out_specs=[pl.BlockSpec((tm, nq), row), pl.BlockSpec((tm, nk), row),
                   pl.BlockSpec((tm, nk), row)],
        out_shape=[jax.ShapeDtypeStruct((t, nq), BF16), jax.ShapeDtypeStruct((t, nk), BF16),
                   jax.ShapeDtypeStruct((t, nk), BF16)],
        compiler_params=_params(1),
        name="attn_qkv",
    )(h, g, wqkv, gq, gk)

    nt = seq // tm
    per = tm // blk
    cur = lambda b, i: (b * nt + i, 0)
    prev = lambda b, i: (jnp.maximum((b * nt + i) * per - 1, 0), 0)
    return pl.pallas_call(
        _attn_kernel,
        grid=(batch, nt),
        in_specs=[pl.BlockSpec(memory_space=pltpu.SMEM),
                  pl.BlockSpec((tm, d), cur), pl.BlockSpec((tm, nq), cur),
                  pl.BlockSpec((tm, nk), cur), pl.BlockSpec((blk, nk), prev),
                  pl.BlockSpec((tm, nk), cur), pl.BlockSpec((blk, nk), prev),
                  _const_spec(wo.shape)],
        out_specs=pl.BlockSpec((tm, d), cur),
        out_shape=jax.ShapeDtypeStruct((t, d), F32),
        scratch_shapes=[pltpu.VMEM((tm, nq), F32)],
        compiler_params=_params(2),
        name="attn_core",
    )(sinks, h, q, k, k, v, v, wo)


CONV_HALO = 32


def _glu_kernel(h_ref, g_ref, w_ref, b_ref, y_ref):
    u = _rms(h_ref[...], g_ref[...]).astype(BF16)
    a = _dot(u, w_ref[...]) + b_ref[...]
    d = y_ref.shape[1]
    y_ref[...] = a[:, :d] * jax.nn.sigmoid(a[:, d:])


def _conv_kernel(h_ref, yc_ref, yp_ref, wdw_ref, bdw_ref, lg_ref, lb_ref, w2_ref, b2_ref, o_ref,
                 ycat_ref, conv_ref, *, row_chunk, col_chunk):
    i = pl.program_id(1)
    tm, d = yc_ref.shape

    @pl.when(i > 0)
    def _():
        ycat_ref[0:CONV_HALO, :] = yp_ref[...]

    @pl.when(i == 0)
    def _():
        ycat_ref[0:CONV_HALO, :] = jnp.zeros((CONV_HALO, d), F32)

    ycat_ref[CONV_HALO:, :] = yc_ref[...]
    first = CONV_HALO - (CONV_WIDTH - 1)
    for r0 in range(0, tm, row_chunk):
        for c0 in range(0, d, col_chunk):
            acc = jnp.zeros((row_chunk, col_chunk), F32)
            for k in range(CONV_WIDTH):
                acc = acc + (wdw_ref[k:k + 1, c0:c0 + col_chunk]
                             * ycat_ref[r0 + first + k:r0 + first + k + row_chunk, c0:c0 + col_chunk])
            conv_ref[r0:r0 + row_chunk, c0:c0 + col_chunk] = acc + bdw_ref[:, c0:c0 + col_chunk]
    c = conv_ref[...]
    mu = jnp.mean(c, axis=-1, keepdims=True)
    cc = c - mu
    var = jnp.mean(cc * cc, axis=-1, keepdims=True)
    y = cc * lax.rsqrt(var + EPS) * lg_ref[...] + lb_ref[...]
    y = y * jax.nn.sigmoid(y)
    o_ref[...] = h_ref[...] + _dot(y.astype(BF16), w2_ref[...]) + b2_ref[...]


def _conformer_layer(h, batch, seq, g, w1, b1, wdw, bdw, lg, lb, w2, b2, *, tm_glu=512, tm_conv=256):
    t, d = h.shape
    row = lambda i: (i, 0)
    tm = min(tm_glu, t)
    y = pl.pallas_call(
        _glu_kernel,
        grid=(t // tm,),
        in_specs=[pl.BlockSpec((tm, d), row), _const_spec(g.shape), _const_spec(w1.shape),
                  _const_spec(b1.shape)],
        out_specs=pl.BlockSpec((tm, d), row),
        out_shape=jax.ShapeDtypeStruct((t, d), F32),
        compiler_params=_params(1),
        name="conf_glu",
    )(h, g, w1, b1)

    tm = min(tm_conv, seq)
    nt = seq // tm
    per = tm // CONV_HALO
    cur = lambda b, i: (b * nt + i, 0)
    prev = lambda b, i: (jnp.maximum((b * nt + i) * per - 1, 0), 0)
    return pl.pallas_call(
        functools.partial(_conv_kernel, row_chunk=64, col_chunk=256),
        grid=(batch, nt),
        in_specs=[pl.BlockSpec((tm, d), cur), pl.BlockSpec((tm, d), cur),
                  pl.BlockSpec((CONV_HALO, d), prev),
                  _const_spec(wdw.shape), _const_spec(bdw.shape), _const_spec(lg.shape),
                  _const_spec(lb.shape), _const_spec(w2.shape), _const_spec(b2.shape)],
        out_specs=pl.BlockSpec((tm, d), cur),
        out_shape=jax.ShapeDtypeStruct((t, d), F32),
        scratch_shapes=[pltpu.VMEM((tm + CONV_HALO, d), F32), pltpu.VMEM((tm, d), F32)],
        compiler_params=_params(2),
        name="conf_conv",
    )(h, y, y, wdw, bdw, lg, lb, w2, b2)


def _mamba_in_kernel(h_ref, g_ref, wz_ref, wx_ref, wdt_ref, wc_ref, bc_ref, dtb_ref,
                     z_ref, xs_ref, b_ref, c_ref, dt_ref, xcat_ref, *, col_chunk):
    i = pl.program_id(1)
    tm = h_ref.shape[0]
    di = xs_ref.shape[1]
    gn = b_ref.shape[1]
    halo = SUBLANES
    u = _rms(h_ref[...], g_ref[...]).astype(BF16)
    z_ref[...] = _dot(u, wz_ref[...])
    dt_ref[...] = jax.nn.softplus(_dot(u, wdt_ref[...]) + dtb_ref[...])

    @pl.when(i == 0)
    def _():
        xcat_ref[0:halo, :] = jnp.zeros((halo, xcat_ref.shape[1]), F32)

    xcat_ref[halo:, :] = _dot(u, wx_ref[...])
    first = halo - (SSM_CONV - 1)
    for c0 in range(0, di + 2 * gn, col_chunk):
        acc = jnp.zeros((tm, col_chunk), F32)
        for k in range(SSM_CONV):
            acc = acc + wc_ref[k:k + 1, c0:c0 + col_chunk] * xcat_ref[first + k:first + k + tm,
                                                                     c0:c0 + col_chunk]
        acc = acc + bc_ref[:, c0:c0 + col_chunk]
        act = acc * jax.nn.sigmoid(acc)
        if c0 < di:
            xs_ref[:, c0:c0 + col_chunk] = act
        elif c0 < di + gn:
            b_ref[:, c0 - di:c0 - di + col_chunk] = act.astype(BF16)
        else:
            c_ref[:, c0 - di - gn:c0 - di - gn + col_chunk] = act.astype(BF16)
    xcat_ref[0:halo, :] = xcat_ref[tm:tm + halo, :]


def _expand_heads(a, g, left):
    lo = jnp.where(left, a[:, 4 * g:4 * g + 1], a[:, 4 * g + 1:4 * g + 2])
    hi = jnp.where(left, a[:, 4 * g + 2:4 * g + 3], a[:, 4 * g + 3:4 * g + 4])
    return jnp.concatenate([lo, hi], axis=1)


def _ssd_kernel(h_ref, z_ref, xs_ref, b_ref, c_ref, dt_ref, alog_ref, dexp_ref, ng_ref, wout_ref,
                o_ref, state_ref, y_ref):
    i = pl.program_id(1)
    tm = h_ref.shape[0]
    L = SSM_CHUNK
    gw = 4 * SSM_HEAD_DIM

    @pl.when(i == 0)
    def _():
        state_ref[...] = jnp.zeros(state_ref.shape, F32)

    r_i = lax.broadcasted_iota(jnp.int32, (L, L), 0)
    c_i = lax.broadcasted_iota(jnp.int32, (L, L), 1)
    causal = r_i >= c_i
    tri = causal.astype(F32)
    left = (lax.broadcasted_iota(jnp.int32, (1, LANES), 1) < SSM_HEAD_DIM)
    a_row = -jnp.exp(alog_ref[...])

    def chunk(c, carry):
        r0 = pl.multiple_of(c * L, L)
        dt = dt_ref[pl.ds(r0, L), :]
        cs = jnp.dot(tri, dt * a_row, precision=lax.Precision.HIGHEST,
                     preferred_element_type=F32)
        cs_t = cs.T
        dt_t = dt.T
        cs_last = cs[L - 1:L, :]
        ecs = jnp.exp(cs)
        wdec = jnp.exp(cs_last - cs) * dt
        for g in range(SSM_GROUPS):
            bg = b_ref[pl.ds(r0, L), g * SSM_STATE:(g + 1) * SSM_STATE]
            cg = c_ref[pl.ds(r0, L), g * SSM_STATE:(g + 1) * SSM_STATE]
            cb = _dot_nt(cg, bg)
            xg = xs_ref[pl.ds(r0, L), g * gw:(g + 1) * gw]
            xgb = xg.astype(BF16)
            ys = []
            for pair in range(2):
                xt = xgb[:, pair * LANES:(pair + 1) * LANES]
                acc = None
                for half in range(2):
                    hh = 4 * g + 2 * pair + half
                    lm = jnp.exp(jnp.where(causal, cs[:, hh:hh + 1] - cs_t[hh:hh + 1, :], -jnp.inf))
                    m = (cb * lm * dt_t[hh:hh + 1, :]).astype(BF16)
                    xm = jnp.where(left if half == 0 else jnp.logical_not(left), xt,
                                   jnp.zeros_like(xt))
                    part = _dot(m, xm)
                    acc = part if acc is None else acc + part
                ys.append(acc)
            y_diag = jnp.concatenate(ys, axis=1)
            st = state_ref[g]
            ecs_e = _expand_heads(ecs, g, left)
            y_off = _dot(cg, st.astype(BF16)) * ecs_e
            y_ref[pl.ds(r0, L), g * gw:(g + 1) * gw] = y_diag + y_off
            xw = (xg * _expand_heads(wdec, g, left)).astype(BF16)
            bg_t = bg.astype(F32).T.astype(BF16)
            state_ref[g] = st * ecs_e[L - 1:L, :] + _dot(bg_t, xw)
        return carry

    lax.fori_loop(0, tm // L, chunk, 0)

    y = y_ref[...] + xs_ref[...] * dexp_ref[...]
    zz = z_ref[...]
    y = y * (zz * jax.nn.sigmoid(zz))
    o_ref[...] = h_ref[...] + _dot(_rms(y, ng_ref[...]).astype(BF16), wout_ref[...])


def _mamba_layer(h, batch, seq, g, wz, wx, wdt, wc, bc, dtb, alog, dexp, ng, wout, *,
                 tm_in=256, tm_ssd=512):
    t, d = h.shape
    di = wz.shape[1]
    gn = SSM_GROUPS * SSM_STATE
    tm = min(tm_in, seq)
    nt = seq // tm
    cur = lambda b, i: (b * nt + i, 0)
    z, xs, bm, cm, dt = pl.pallas_call(
        functools.partial(_mamba_in_kernel, col_chunk=512),
        grid=(batch, nt),
        in_specs=[pl.BlockSpec((tm, d), cur), _const_spec(g.shape), _const_spec(wz.shape),
                  _const_spec(wx.shape), _const_spec(wdt.shape), _const_spec(wc.shape),
                  _const_spec(bc.shape), _const_spec(dtb.shape)],
        out_specs=[pl.BlockSpec((tm, di), cur), pl.BlockSpec((tm, di), cur),
                   pl.BlockSpec((tm, gn), cur), pl.BlockSpec((tm, gn), cur),
                   pl.BlockSpec((tm, LANES), cur)],
        out_shape=[jax.ShapeDtypeStruct((t, di), F32), jax.ShapeDtypeStruct((t, di), F32),
                   jax.ShapeDtypeStruct((t, gn), BF16), jax.ShapeDtypeStruct((t, gn), BF16),
                   jax.ShapeDtypeStruct((t, LANES), F32)],
        scratch_shapes=[pltpu.VMEM((tm + SUBLANES, di + 2 * gn), F32)],
        compiler_params=_params(2),
        name="mamba_in",
    )(h, g, wz, wx, wdt, wc, bc, dtb)

    tm = min(tm_ssd, seq)
    nt = seq // tm
    cur = lambda b, i: (b * nt + i, 0)
    return pl.pallas_call(
        _ssd_kernel,
        grid=(batch, nt),
        in_specs=[pl.BlockSpec((tm, d), cur), pl.BlockSpec((tm, di), cur),
                  pl.BlockSpec((tm, di), cur), pl.BlockSpec((tm, gn), cur),
                  pl.BlockSpec((tm, gn), cur), pl.BlockSpec((tm, LANES), cur),
                  _const_spec(alog.shape), _const_spec(dexp.shape), _const_spec(ng.shape),
                  _const_spec(wout.shape)],
        out_specs=pl.BlockSpec((tm, d), cur),
        out_shape=jax.ShapeDtypeStruct((t, d), F32),
        scratch_shapes=[pltpu.VMEM((SSM_GROUPS, SSM_STATE, 4 * SSM_HEAD_DIM), F32),
                        pltpu.VMEM((tm, di), F32)],
        compiler_params=_params(2),
        name="mamba_ssd",
    )(h, z, xs, bm, cm, dt, alog, dexp, ng, wout)


def _row(v):
    return v.reshape(1, -1).astype(F32)


def _pad_cols(a, n):
    return jnp.pad(a, ((0, 0), (0, n - a.shape[1])))


def kernel(x, p, mix_norm_g, mlp_norm_g, ple_norm_g, a_wqkv, a_q_norm_g, a_k_norm_g, a_sinks, a_wo, b_w_pw1, b_b_pw1, b_w_dw, b_b_dw, b_ln_g, b_ln_b, b_w_pw2, b_b_pw2, c_w_in, c_w_conv, c_b_conv, c_dt_bias, c_A_log, c_D, c_norm_g, c_w_out, m_w1, m_w2, ple_w_proj, ple_w_gate):
    batch, seq, d = x.shape
    depth = p.shape[0]
    t = batch * seq
    h = x.reshape(t, d)
    for i in range(depth):
        kind, j = i % N_MIXERS, i // N_MIXERS
        g = _row(mix_norm_g[i])
        if kind == 0:
            scale = ATTN_HEAD_DIM ** -0.5
            gq = _row(jnp.tile(a_q_norm_g[j] * scale, 2))
            gk = _row(jnp.tile(a_k_norm_g[j], 2))
            h = _attention_layer(h, batch, seq, g, a_wqkv[j].astype(BF16), gq, gk,
                                 a_sinks[j].astype(F32), a_wo[j].astype(BF16))
        elif kind == 1:
            wdw = jnp.pad(b_w_dw[j], ((0, CONV_HALO - CONV_WIDTH), (0, 0)))
            h = _conformer_layer(h, batch, seq, g, b_w_pw1[j].astype(BF16), _row(b_b_pw1[j]),
                                 wdw, _row(b_b_dw[j]), _row(b_ln_g[j]), _row(b_ln_b[j]),
                                 b_w_pw2[j].astype(BF16), _row(b_b_pw2[j]))
        else:
            di = c_w_out.shape[1]
            gn = SSM_GROUPS * SSM_STATE
            w_in = c_w_in[j]
            wz = w_in[:, :di].astype(BF16)
            wx = w_in[:, di:2 * di + 2 * gn].astype(BF16)
            wdt = _pad_cols(w_in[:, 2 * di + 2 * gn:], LANES).astype(BF16)
            wc = jnp.pad(c_w_conv[j], ((0, SUBLANES - SSM_CONV), (0, 0)))
            dtb = _pad_cols(_row(c_dt_bias[j]), LANES)
            alog = _pad_cols(_row(c_A_log[j]), LANES)
            dexp = _row(jnp.repeat(c_D[j], SSM_HEAD_DIM))
            h = _mamba_layer(h, batch, seq, g, wz, wx, wdt, wc, _row(c_b_conv[j]), dtb, alog,
                             dexp, _row(c_norm_g[j]), c_w_out[j].astype(BF16))
        h = _mlp_ple(h, p[i].reshape(t, -1), _row(mlp_norm_g[i]), _row(ple_norm_g[i]),
                     m_w1[i].astype(BF16), m_w2[i].astype(BF16),
                     ple_w_gate[i].astype(BF16), ple_w_proj[i].astype(BF16))
    return h.reshape(batch, seq, d)
```

```python
import functools

import jax
import jax.numpy as jnp
from jax import lax
from jax.experimental import pallas as pl
from jax.experimental.pallas import tpu as pltpu

F32 = jnp.float32
BF16 = jnp.bfloat16
EPS = 1e-6

N_MIXERS = 3
ATTN_HEADS = 16
ATTN_KV_HEADS = 4
ATTN_HEAD_DIM = 64
ATTN_BLOCK = 128
CONV_WIDTH = 31
SSM_HEAD_DIM = 64
SSM_GROUPS = 8
SSM_STATE = 128
SSM_CONV = 4
SSM_CHUNK = 128

LANES = 128
SUBLANES = 8
VMEM_LIMIT_BYTES = 56 * 1024 * 1024


def _params(n_grid_dims):
    return pltpu.CompilerParams(
        dimension_semantics=("arbitrary",) * n_grid_dims,
        vmem_limit_bytes=VMEM_LIMIT_BYTES)


def _const_spec(shape):
    nd = len(shape)
    return pl.BlockSpec(shape, lambda *_: (0,) * nd, pipeline_mode=pl.Buffered(1))


def _rms(x, g):
    return x * lax.rsqrt(jnp.mean(x * x, axis=-1, keepdims=True) + EPS) * g


def _dot(a, b):
    return jnp.dot(a, b, preferred_element_type=F32)


def _dot_nt(a, b):
    return lax.dot_general(a, b, (((1,), (1,)), ((), ())), preferred_element_type=F32)


def _left_half():
    return lax.broadcasted_iota(jnp.int32, (1, LANES), 1) < (LANES // 2)


def _mlp_ple_kernel(h_ref, p_ref, g1_ref, g2_ref, w1_ref, w2_ref, wg_ref, wp_ref, o_ref, *,
                    hidden_chunk):
    x = h_ref[...]
    u = _rms(x, g1_ref[...]).astype(BF16)
    acc = jnp.zeros_like(x)
    for c in range(0, w1_ref.shape[1], hidden_chunk):
        a = _dot(u, w1_ref[:, c:c + hidden_chunk])
        a = jnp.square(jnp.maximum(a, 0.0)).astype(BF16)
        acc = acc + _dot(a, w2_ref[c:c + hidden_chunk, :])
    h2 = x + acc
    u2 = _rms(h2, g2_ref[...]).astype(BF16)
    gate = jax.nn.sigmoid(_dot(u2, wg_ref[...]))
    proj = _dot(p_ref[...].astype(BF16), wp_ref[...])
    o_ref[...] = h2 + gate * proj


def _mlp_ple(h, p, layer, g1, g2, w1, w2, wg, wp, *, tm=512, hidden_chunk=1024):
    t, d = h.shape
    tm = min(tm, t)
    row = lambda i: (i, 0)
    return pl.pallas_call(
        functools.partial(_mlp_ple_kernel, hidden_chunk=hidden_chunk),
        grid=(t // tm,),
        in_specs=[pl.BlockSpec((tm, d), row),
                  pl.BlockSpec((None, tm, p.shape[2]), lambda i: (layer, i, 0)),
                  _const_spec(g1.shape), _const_spec(g2.shape), _const_spec(w1.shape),
                  _const_spec(w2.shape), _const_spec(wg.shape), _const_spec(wp.shape)],
        out_specs=pl.BlockSpec((tm, d), row),
        out_shape=jax.ShapeDtypeStruct((t, d), F32),
        compiler_params=_params(1),
        name="mlp_ple",
    )(h, p, g1, g2, w1, w2, wg, wp)


def _qkv_kernel(h_ref, g_ref, w_ref, gq_ref, gk_ref, q_ref, ke_ref, ko_ref, v2_ref):
    hd = ATTN_HEAD_DIM
    nq = q_ref.shape[1]
    nk = ATTN_KV_HEADS * hd
    u = _rms(h_ref[...], g_ref[...]).astype(BF16)
    qkv = _dot(u, w_ref[...])
    left = _left_half()

    def pair_norm(t, g):
        sq = t * t
        ml = jnp.sum(jnp.where(left, sq, 0.0), axis=-1, keepdims=True) * (1.0 / hd)
        mr = jnp.sum(jnp.where(left, 0.0, sq), axis=-1, keepdims=True) * (1.0 / hd)
        inv = jnp.where(left, lax.rsqrt(ml + EPS), lax.rsqrt(mr + EPS))
        return t * inv * g

    for c in range(0, nq, LANES):
        q_ref[:, c:c + LANES] = pair_norm(qkv[:, c:c + LANES], gq_ref[...]).astype(BF16)
    for j in range(nk // LANES):
        c = j * LANES
        t = pair_norm(qkv[:, nq + c:nq + c + LANES], gk_ref[...])
        r = pltpu.roll(t, hd, axis=1)
        zero = jnp.zeros_like(t)
        first, second = 2 * j * LANES, (2 * j + 1) * LANES
        ke_ref[:, first:first + LANES] = jnp.where(left, t, zero).astype(BF16)
        ke_ref[:, second:second + LANES] = jnp.where(left, r, zero).astype(BF16)
        ko_ref[:, first:first + LANES] = jnp.where(left, zero, r).astype(BF16)
        ko_ref[:, second:second + LANES] = jnp.where(left, zero, t).astype(BF16)
        tv = qkv[:, nq + nk + c:nq + nk + c + LANES]
        rv = pltpu.roll(tv, hd, axis=1)
        v2_ref[:, first:first + LANES] = jnp.where(left, tv, rv).astype(BF16)
        v2_ref[:, second:second + LANES] = jnp.where(left, rv, tv).astype(BF16)


def _attn_kernel(sink_ref, bias_ref, h_ref, q_ref, kec_ref, kep_ref, koc_ref, kop_ref, vc_ref, vp_ref,
                 wo_ref, o_ref, ke_ref, ko_ref, v_ref, oacc_ref):
    blk = ATTN_BLOCK
    group = ATTN_HEADS // ATTN_KV_HEADS
    i = pl.program_id(1)
    tq = q_ref.shape[0]
    ke_ref[0:blk, :] = kep_ref[...]
    ke_ref[blk:, :] = kec_ref[...]
    ko_ref[0:blk, :] = kop_ref[...]
    ko_ref[blk:, :] = koc_ref[...]
    v_ref[0:blk, :] = vp_ref[...]
    v_ref[blk:, :] = vc_ref[...]

    row = lax.broadcasted_iota(jnp.int32, (blk, blk), 0)
    col = lax.broadcasted_iota(jnp.int32, (blk, blk), 1)
    lower = col <= row
    no_prev = (col - row) > jnp.where(i > 0, blk, 0)
    left = _left_half()
    for b0 in range(0, tq, blk):
        for kv in range(ATTN_KV_HEADS):
            lanes = slice(kv * LANES, (kv + 1) * LANES)
            qt = jnp.concatenate(
                [q_ref[b0:b0 + blk, (2 * kv) * LANES:(2 * kv + 1) * LANES],
                 q_ref[b0:b0 + blk, (2 * kv + 1) * LANES:(2 * kv + 2) * LANES]], axis=0)
            s_even = _dot_nt(qt, ke_ref[b0:b0 + 2 * blk, lanes])
            s_odd = _dot_nt(qt, ko_ref[b0:b0 + 2 * blk, lanes])
            p_cur, p_prev, r_den = [], [], []
            for g in range(group):
                src = s_even if g % 2 == 0 else s_odd
                r0 = (g // 2) * blk
                hh = kv * group + g
                s = jnp.where(lower, src[r0:r0 + blk, blk:], src[r0:r0 + blk, :blk]) + bias_ref[hh]
                if b0 == 0:
                    s = jnp.where(no_prev, -jnp.inf, s)
                sink = sink_ref[hh]
                m = jnp.maximum(jnp.max(s, axis=-1, keepdims=True), sink)
                e = jnp.exp(s - m)
                r_den.append(1.0 / (jnp.sum(e, axis=-1, keepdims=True) + jnp.exp(sink - m)))
                p_cur.append(jnp.where(lower, e, 0.0).astype(BF16))
                p_prev.append(jnp.where(lower, 0.0, e).astype(BF16))
            o4 = (_dot(jnp.concatenate(p_cur, axis=0), v_ref[b0 + blk:b0 + 2 * blk, lanes])
                  + _dot(jnp.concatenate(p_prev, axis=0), v_ref[b0:b0 + blk, lanes]))
            for pair in range(group // 2):
                ga, gb = 2 * pair, 2 * pair + 1
                oa = o4[ga * blk:(ga + 1) * blk, :] * r_den[ga]
                ob = o4[gb * blk:(gb + 1) * blk, :] * r_den[gb]
                c0 = (2 * kv + pair) * LANES
                oacc_ref[b0:b0 + blk, c0:c0 + LANES] = jnp.where(left, oa, ob).astype(BF16)
    o_ref[...] = h_ref[...] + _dot(oacc_ref[...], wo_ref[...])


def _alibi_bias():
    blk = ATTN_BLOCK
    row = jnp.arange(blk, dtype=jnp.int32)[:, None]
    col = jnp.arange(blk, dtype=jnp.int32)[None, :]
    rel = jnp.where(col <= row, row - col, row - col + blk).astype(F32)
    slopes = jnp.exp2(-8.0 * (jnp.arange(ATTN_HEADS, dtype=F32) + 1.0) / ATTN_HEADS)
    return -(slopes[:, None, None] * rel[None])


def _attention_layer(h, batch, seq, g, wqkv, gq, gk, sinks, wo, *, tm=512):
    t, d = h.shape
    hd, blk = ATTN_HEAD_DIM, ATTN_BLOCK
    nq, nkp = ATTN_HEADS * hd, ATTN_KV_HEADS * LANES
    tm = min(tm, seq)
    row = lambda i: (i, 0)
    kv_shape = jax.ShapeDtypeStruct((t, nkp), BF16)
    q, ke, ko, v2 = pl.pallas_call(
        _qkv_kernel,
        grid=(t // tm,),
        in_specs=[pl.BlockSpec((tm, d), row), _const_spec(g.shape), _const_spec(wqkv.shape),
                  _const_spec(gq.shape), _const_spec(gk.shape)],
        out_specs=[pl.BlockSpec((tm, nq), row)] + [pl.BlockSpec((tm, nkp), row)] * 3,
        out_shape=[jax.ShapeDtypeStruct((t, nq), BF16), kv_shape, kv_shape, kv_shape],
        compiler_params=_params(1),
        name="attn_qkv",
    )(h, g, wqkv, gq, gk)

    bias = _alibi_bias()
    nt = seq // tm
    per = tm // blk
    cur = lambda b, i: (b * nt + i, 0)
    prev = lambda b, i: (jnp.maximum((b * nt + i) * per - 1, 0), 0)
    kv_cur, kv_prev = pl.BlockSpec((tm, nkp), cur), pl.BlockSpec((blk, nkp), prev)
    return pl.pallas_call(
        _attn_kernel,
        grid=(batch, nt),
        in_specs=[pl.BlockSpec(memory_space=pltpu.SMEM), _const_spec(bias.shape),
                  pl.BlockSpec((tm, d), cur), pl.BlockSpec((tm, nq), cur),
                  kv_cur, kv_prev, kv_cur, kv_prev, kv_cur, kv_prev,
                  _const_spec(wo.shape)],
        out_specs=pl.BlockSpec((tm, d), cur),
        out_shape=jax.ShapeDtypeStruct((t, d), F32),
        scratch_shapes=[pltpu.VMEM((tm + blk, nkp), BF16), pltpu.VMEM((tm + blk, nkp), BF16),
                        pltpu.VMEM((tm + blk, nkp), BF16), pltpu.VMEM((tm, nq), BF16)],
        compiler_params=_params(2),
        name="attn_core",
    )(sinks, bias, h, q, ke, ke, ko, ko, v2, v2, wo)


CONV_HALO = 32


def _conformer_kernel(h_ref, g_ref, w1_ref, b1_ref, wdw_ref, bdw_ref, lg_ref, lb_ref, w2_ref, b2_ref,
                      o_ref, ycat_ref, ysh_ref, conv_ref, *, row_chunk, col_chunk):
    i = pl.program_id(1)
    tm, d = h_ref.shape
    halo = CONV_HALO

    @pl.when(i == 0)
    def _():
        ycat_ref[0:halo, :] = jnp.zeros((halo, d), F32)

    x = h_ref[...]
    u = _rms(x, g_ref[...]).astype(BF16)
    a = _dot(u, w1_ref[...]) + b1_ref[...]
    ycat_ref[halo:, :] = a[:, :d] * jax.nn.sigmoid(a[:, d:])

    span = tm + halo - SUBLANES
    for s in range(1, SUBLANES):
        ysh_ref[s - 1] = ycat_ref[s:s + span, :]
    first = halo - (CONV_WIDTH - 1)
    for r0 in range(0, tm, row_chunk):
        for c0 in range(0, d, col_chunk):
            cols = slice(c0, c0 + col_chunk)
            acc = jnp.zeros((row_chunk, col_chunk), F32)
            for k in range(CONV_WIDTH):
                phase = (first + k) % SUBLANES
                base = r0 + first + k - phase
                if phase == 0:
                    tap = ycat_ref[base:base + row_chunk, cols]
                else:
                    tap = ysh_ref[phase - 1, base:base + row_chunk, cols]
                acc = acc + wdw_ref[k:k + 1, cols] * tap
            conv_ref[r0:r0 + row_chunk, cols] = acc + bdw_ref[:, cols]
    ycat_ref[0:halo, :] = ycat_ref[tm:tm + halo, :]

    c = conv_ref[...]
    mu = jnp.mean(c, axis=-1, keepdims=True)
    cc = c - mu
    var = jnp.mean(cc * cc, axis=-1, keepdims=True)
    y = cc * lax.rsqrt(var + EPS) * lg_ref[...] + lb_ref[...]
    y = y * jax.nn.sigmoid(y)
    o_ref[...] = x + _dot(y.astype(BF16), w2_ref[...]) + b2_ref[...]


def _conformer_layer(h, batch, seq, g, w1, b1, wdw, bdw, lg, lb, w2, b2, *, tm=256):
    t, d = h.shape
    tm = min(tm, seq)
    nt = seq // tm
    cur = lambda b, i: (b * nt + i, 0)
    return pl.pallas_call(
        functools.partial(_conformer_kernel, row_chunk=64, col_chunk=256),
        grid=(batch, nt),
        in_specs=[pl.BlockSpec((tm, d), cur), _const_spec(g.shape), _const_spec(w1.shape),
                  _const_spec(b1.shape), _const_spec(wdw.shape), _const_spec(bdw.shape),
                  _const_spec(lg.shape), _const_spec(lb.shape), _const_spec(w2.shape),
                  _const_spec(b2.shape)],
        out_specs=pl.BlockSpec((tm, d), cur),
        out_shape=jax.ShapeDtypeStruct((t, d), F32),
        scratch_shapes=[pltpu.VMEM((tm + CONV_HALO, d), F32),
                        pltpu.VMEM((SUBLANES - 1, tm + CONV_HALO - SUBLANES, d), F32),
                        pltpu.VMEM((tm, d), F32)],
        compiler_params=_params(2),
        name="conformer",
    )(h, g, w1, b1, wdw, bdw, lg, lb, w2, b2)


def _mamba_in_kernel(h_ref, g_ref, wz_ref, wx_ref, wdt_ref, wc_ref, bc_ref, dtb_ref,
                     z_ref, xs_ref, b_ref, c_ref, dt_ref, xcat_ref, *, col_chunk):
    i = pl.program_id(1)
    tm = h_ref.shape[0]
    di = xs_ref.shape[1]
    gn = b_ref.shape[1]
    halo = SUBLANES
    u = _rms(h_ref[...], g_ref[...]).astype(BF16)
    z_ref[...] = _dot(u, wz_ref[...])
    dt_ref[...] = jax.nn.softplus(_dot(u, wdt_ref[...]) + dtb_ref[...])

    @pl.when(i == 0)
    def _():
        xcat_ref[0:halo, :] = jnp.zeros((halo, xcat_ref.shape[1]), F32)

    xcat_ref[halo:, :] = _dot(u, wx_ref[...])
    first = halo - (SSM_CONV - 1)
    for c0 in range(0, di + 2 * gn, col_chunk):
        acc = jnp.zeros((tm, col_chunk), F32)
        for k in range(SSM_CONV):
            acc = acc + wc_ref[k:k + 1, c0:c0 + col_chunk] * xcat_ref[first + k:first + k + tm,
                                                                     c0:c0 + col_chunk]
        acc = acc + bc_ref[:, c0:c0 + col_chunk]
        act = acc * jax.nn.sigmoid(acc)
        if c0 < di:
            xs_ref[:, c0:c0 + col_chunk] = act
        elif c0 < di + gn:
            b_ref[:, c0 - di:c0 - di + col_chunk] = act.astype(BF16)
        else:
            c_ref[:, c0 - di - gn:c0 - di - gn + col_chunk] = act.astype(BF16)
    xcat_ref[0:halo, :] = xcat_ref[tm:tm + halo, :]


def _expand_heads(a, g, left):
    lo = jnp.where(left, a[:, 4 * g:4 * g + 1], a[:, 4 * g + 1:4 * g + 2])
    hi = jnp.where(left, a[:, 4 * g + 2:4 * g + 3], a[:, 4 * g + 3:4 * g + 4])
    return jnp.concatenate([lo, hi], axis=1)


def _ssd_kernel(h_ref, z_ref, xs_ref, b_ref, c_ref, dt_ref, alog_ref, dexp_ref, ng_ref, wout_ref,
                o_ref, state_ref, y_ref):
    i = pl.program_id(1)
    tm = h_ref.shape[0]
    L = SSM_CHUNK
    gw = 4 * SSM_HEAD_DIM

    @pl.when(i == 0)
    def _():
        state_ref[...] = jnp.zeros(state_ref.shape, F32)

    r_i = lax.broadcasted_iota(jnp.int32, (L, L), 0)
    c_i = lax.broadcasted_iota(jnp.int32, (L, L), 1)
    causal = r_i >= c_i
    tri = causal.astype(F32)
    left = _left_half()
    a_row = -jnp.exp(alog_ref[...])

    def chunk(c, carry):
        r0 = pl.multiple_of(c * L, L)
        dt = dt_ref[pl.ds(r0, L), :]
        cs = jnp.dot(tri, dt * a_row, precision=lax.Precision.HIGHEST,
                     preferred_element_type=F32)
        cs_t = cs.T
        dt_t = dt.T
        cs_last = cs[L - 1:L, :]
        ecs = jnp.exp(cs)
        wdec = jnp.exp(cs_last - cs) * dt
        for g in range(SSM_GROUPS):
            bg = b_ref[pl.ds(r0, L), g * SSM_STATE:(g + 1) * SSM_STATE]
            cg = c_ref[pl.ds(r0, L), g * SSM_STATE:(g + 1) * SSM_STATE]
            cb = _dot_nt(cg, bg)
            xg = xs_ref[pl.ds(r0, L), g * gw:(g + 1) * gw]
            xgb = xg.astype(BF16)
            ys = []
            for pair in range(2):
                xt = xgb[:, pair * LANES:(pair + 1) * LANES]
                acc = None
                for half in range(2):
                    hh = 4 * g + 2 * pair + half
                    lm = jnp.exp(jnp.where(causal, cs[:, hh:hh + 1] - cs_t[hh:hh + 1, :], -jnp.inf))
                    m = (cb * lm * dt_t[hh:hh + 1, :]).astype(BF16)
                    xm = jnp.where(left if half == 0 else jnp.logical_not(left), xt,
                                   jnp.zeros_like(xt))
                    part = _dot(m, xm)
                    acc = part if acc is None else acc + part
                ys.append(acc)
            y_diag = jnp.concatenate(ys, axis=1)
            st = state_ref[g]
            ecs_e = _expand_heads(ecs, g, left)
            y_off = _dot(cg, st.astype(BF16)) * ecs_e
            y_ref[pl.ds(r0, L), g * gw:(g + 1) * gw] = y_diag + y_off
            xw = (xg * _expand_heads(wdec, g, left)).astype(BF16)
            bg_t = bg.astype(F32).T.astype(BF16)
            state_ref[g] = st * ecs_e[L - 1:L, :] + _dot(bg_t, xw)
        return carry

    lax.fori_loop(0, tm // L, chunk, 0)

    y = y_ref[...] + xs_ref[...] * dexp_ref[...]
    zz = z_ref[...]
    y = y * (zz * jax.nn.sigmoid(zz))
    o_ref[...] = h_ref[...] + _dot(_rms(y, ng_ref[...]).astype(BF16), wout_ref[...])


def _mamba_layer(h, batch, seq, g, wz, wx, wdt, wc, bc, dtb, alog, dexp, ng, wout, *,
                 tm_in=256, tm_ssd=512):
    t, d = h.shape
    di = wz.shape[1]
    gn = SSM_GROUPS * SSM_STATE
    tm = min(tm_in, seq)
    nt = seq // tm
    cur = lambda b, i: (b * nt + i, 0)
    z, xs, bm, cm, dt = pl.pallas_call(
        functools.partial(_mamba_in_kernel, col_chunk=512),
        grid=(batch, nt),
        in_specs=[pl.BlockSpec((tm, d), cur), _const_spec(g.shape), _const_spec(wz.shape),
                  _const_spec(wx.shape), _const_spec(wdt.shape), _const_spec(wc.shape),
                  _const_spec(bc.shape), _const_spec(dtb.shape)],
        out_specs=[pl.BlockSpec((tm, di), cur), pl.BlockSpec((tm, di), cur),
                   pl.BlockSpec((tm, gn), cur), pl.BlockSpec((tm, gn), cur),
                   pl.BlockSpec((tm, LANES), cur)],
        out_shape=[jax.ShapeDtypeStruct((t, di), F32), jax.ShapeDtypeStruct((t, di), F32),
                   jax.ShapeDtypeStruct((t, gn), BF16), jax.ShapeDtypeStruct((t, gn), BF16),
                   jax.ShapeDtypeStruct((t, LANES), F32)],
        scratch_shapes=[pltpu.VMEM((tm + SUBLANES, di + 2 * gn), F32)],
        compiler_params=_params(2),
        name="mamba_in",
    )(h, g, wz, wx, wdt, wc, bc, dtb)

    tm = min(tm_ssd, seq)
    nt = seq // tm
    cur = lambda b, i: (b * nt + i, 0)
    return pl.pallas_call(
        _ssd_kernel,
        grid=(batch, nt),
        in_specs=[pl.BlockSpec((tm, d), cur), pl.BlockSpec((tm, di), cur),
                  pl.BlockSpec((tm, di), cur), pl.BlockSpec((tm, gn), cur),
                  pl.BlockSpec((tm, gn), cur), pl.BlockSpec((tm, LANES), cur),
                  _const_spec(alog.shape), _const_spec(dexp.shape), _const_spec(ng.shape),
                  _const_spec(wout.shape)],
        out_specs=pl.BlockSpec((tm, d), cur),
        out_shape=jax.ShapeDtypeStruct((t, d), F32),
        scratch_shapes=[pltpu.VMEM((SSM_GROUPS, SSM_STATE, 4 * SSM_HEAD_DIM), F32),
                        pltpu.VMEM((tm, di), F32)],
        compiler_params=_params(2),
        name="mamba_ssd",
    )(h, z, xs, bm, cm, dt, alog, dexp, ng, wout)


def _row(v):
    return v.reshape(1, -1).astype(F32)


def _pad_cols(a, n):
    return jnp.pad(a, ((0, 0), (0, n - a.shape[1])))


def kernel(x, p, mix_norm_g, mlp_norm_g, ple_norm_g, a_wqkv, a_q_norm_g, a_k_norm_g, a_sinks, a_wo, b_w_pw1, b_b_pw1, b_w_dw, b_b_dw, b_ln_g, b_ln_b, b_w_pw2, b_b_pw2, c_w_in, c_w_conv, c_b_conv, c_dt_bias, c_A_log, c_D, c_norm_g, c_w_out, m_w1, m_w2, ple_w_proj, ple_w_gate):
    batch, seq, d = x.shape
    depth = p.shape[0]
    t = batch * seq
    h = x.reshape(t, d)
    p = p.reshape(depth, t, -1)
    for i in range(depth):
        kind, j = i % N_MIXERS, i // N_MIXERS
        g = _row(mix_norm_g[i])
        if kind == 0:
            scale = ATTN_HEAD_DIM ** -0.5
            gq = _row(jnp.tile(a_q_norm_g[j] * scale, 2))
            gk = _row(jnp.tile(a_k_norm_g[j], 2))
            h = _attention_layer(h, batch, seq, g, a_wqkv[j].astype(BF16), gq, gk,
                                 a_sinks[j].astype(F32), a_wo[j].astype(BF16))
        elif kind == 1:
            wdw = jnp.pad(b_w_dw[j], ((0, CONV_HALO - CONV_WIDTH), (0, 0)))
            h = _conformer_layer(h, batch, seq, g, b_w_pw1[j].astype(BF16), _row(b_b_pw1[j]),
                                 wdw, _row(b_b_dw[j]), _row(b_ln_g[j]), _row(b_ln_b[j]),
                                 b_w_pw2[j].astype(BF16), _row(b_b_pw2[j]))
        else:
            di = c_w_out.shape[1]
            gn = SSM_GROUPS * SSM_STATE
            w_in = c_w_in[j]
            wz = w_in[:, :di].astype(BF16)
            wx = w_in[:, di:2 * di + 2 * gn].astype(BF16)
            wdt = _pad_cols(w_in[:, 2 * di + 2 * gn:], LANES).astype(BF16)
            wc = jnp.pad(c_w_conv[j], ((0, SUBLANES - SSM_CONV), (0, 0)))
            dtb = _pad_cols(_row(c_dt_bias[j]), LANES)
            alog = _pad_cols(_row(c_A_log[j]), LANES)
            dexp = _row(jnp.repeat(c_D[j], SSM_HEAD_DIM))
            h = _mamba_layer(h, batch, seq, g, wz, wx, wdt, wc, _row(c_b_conv[j]), dtb, alog,
                             dexp, _row(c_norm_g[j]), c_w_out[j].astype(BF16))
        h = _mlp_ple(h, p, i, _row(mlp_norm_g[i]), _row(ple_norm_g[i]),
                     m_w1[i].astype(BF16), m_w2[i].astype(BF16),
                     ple_w_gate[i].astype(BF16), ple_w_proj[i].astype(BF16))
    return h.reshape(batch, seq, d)
```

```python
import functools

import jax
import jax.numpy as jnp
from jax import lax
from jax.experimental import pallas as pl
from jax.experimental.pallas import tpu as pltpu

F32 = jnp.float32
BF16 = jnp.bfloat16
EPS = 1e-6

N_MIXERS = 3
ATTN_HEADS = 16
ATTN_KV_HEADS = 4
ATTN_HEAD_DIM = 64
ATTN_BLOCK = 128
CONV_WIDTH = 31
SSM_HEAD_DIM = 64
SSM_GROUPS = 8
SSM_STATE = 128
SSM_CONV = 4
SSM_CHUNK = 128

LANES = 128
SUBLANES = 8
VMEM_LIMIT_BYTES = 56 * 1024 * 1024


def _params(n_grid_dims):
    return pltpu.CompilerParams(
        dimension_semantics=("arbitrary",) * n_grid_dims,
        vmem_limit_bytes=VMEM_LIMIT_BYTES)


def _const_spec(shape):
    nd = len(shape)
    return pl.BlockSpec(shape, lambda *_: (0,) * nd, pipeline_mode=pl.Buffered(1))


def _rms(x, g):
    return x * lax.rsqrt(jnp.mean(x * x, axis=-1, keepdims=True) + EPS) * g


def _dot(a, b):
    return jnp.dot(a, b, preferred_element_type=F32)


def _dot_nt(a, b):
    return lax.dot_general(a, b, (((1,), (1,)), ((), ())), preferred_element_type=F32)


def _left_half():
    return lax.broadcasted_iota(jnp.int32, (1, LANES), 1) < (LANES // 2)


def _mlp_ple_kernel(h_ref, p_ref, g1_ref, g2_ref, w1_ref, w2_ref, wg_ref, wp_ref, o_ref, *,
                    hidden_chunk):
    x = h_ref[...]
    u = _rms(x, g1_ref[...]).astype(BF16)
    acc = jnp.zeros_like(x)
    for c in range(0, w1_ref.shape[1], hidden_chunk):
        a = _dot(u, w1_ref[:, c:c + hidden_chunk])
        a = jnp.square(jnp.maximum(a, 0.0)).astype(BF16)
        acc = acc + _dot(a, w2_ref[c:c + hidden_chunk, :])
    h2 = x + acc
    u2 = _rms(h2, g2_ref[...]).astype(BF16)
    gate = jax.nn.sigmoid(_dot(u2, wg_ref[...]))
    proj = _dot(p_ref[...].astype(BF16), wp_ref[...])
    o_ref[...] = h2 + gate * proj


def _mlp_ple(h, p, layer, g1, g2, w1, w2, wg, wp, *, tm=512, hidden_chunk=1024):
    t, d = h.shape
    tm = min(tm, t)
    row = lambda i: (i, 0)
    return pl.pallas_call(
        functools.partial(_mlp_ple_kernel, hidden_chunk=hidden_chunk),
        grid=(t // tm,),
        in_specs=[pl.BlockSpec((tm, d), row),
                  pl.BlockSpec((None, tm, p.shape[2]), lambda i: (layer, i, 0)),
                  _const_spec(g1.shape), _const_spec(g2.shape), _const_spec(w1.shape),
                  _const_spec(w2.shape), _const_spec(wg.shape), _const_spec(wp.shape)],
        out_specs=pl.BlockSpec((tm, d), row),
        out_shape=jax.ShapeDtypeStruct((t, d), F32),
        compiler_params=_params(1),
        name="mlp_ple",
    )(h, p, g1, g2, w1, w2, wg, wp)


def _qkv_kernel(h_ref, g_ref, w_ref, gq_ref, gk_ref, q_ref, ke_ref, ko_ref, v2_ref):
    hd = ATTN_HEAD_DIM
    nq = q_ref.shape[1]
    nk = ATTN_KV_HEADS * hd
    u = _rms(h_ref[...], g_ref[...]).astype(BF16)
    qkv = _dot(u, w_ref[...])
    left = _left_half()

    def pair_norm(t, g):
        sq = t * t
        ml = jnp.sum(jnp.where(left, sq, 0.0), axis=-1, keepdims=True) * (1.0 / hd)
        mr = jnp.sum(jnp.where(left, 0.0, sq), axis=-1, keepdims=True) * (1.0 / hd)
        inv = jnp.where(left, lax.rsqrt(ml + EPS), lax.rsqrt(mr + EPS))
        return t * inv * g

    for c in range(0, nq, LANES):
        q_ref[:, c:c + LANES] = pair_norm(qkv[:, c:c + LANES], gq_ref[...]).astype(BF16)
    for j in range(nk // LANES):
        c = j * LANES
        t = pair_norm(qkv[:, nq + c:nq + c + LANES], gk_ref[...])
        r = pltpu.roll(t, hd, axis=1)
        zero = jnp.zeros_like(t)
        first, second = 2 * j * LANES, (2 * j + 1) * LANES
        ke_ref[:, first:first + LANES] = jnp.where(left, t, zero).astype(BF16)
        ke_ref[:, second:second + LANES] = jnp.where(left, r, zero).astype(BF16)
        ko_ref[:, first:first + LANES] = jnp.where(left, zero, r).astype(BF16)
        ko_ref[:, second:second + LANES] = jnp.where(left, zero, t).astype(BF16)
        tv = qkv[:, nq + nk + c:nq + nk + c + LANES]
        rv = pltpu.roll(tv, hd, axis=1)
        v2_ref[:, first:first + LANES] = jnp.where(left, tv, rv).astype(BF16)
        v2_ref[:, second:second + LANES] = jnp.where(left, rv, tv).astype(BF16)


def _attn_kernel(sink_ref, bias_ref, h_ref, q_ref, kec_ref, kep_ref, koc_ref, kop_ref, vc_ref, vp_ref,
                 wo_ref, o_ref, ke_ref, ko_ref, v_ref, oacc_ref):
    blk = ATTN_BLOCK
    group = ATTN_HEADS // ATTN_KV_HEADS
    i = pl.program_id(1)
    tq = q_ref.shape[0]
    ke_ref[0:blk, :] = kep_ref[...]
    ke_ref[blk:, :] = kec_ref[...]
    ko_ref[0:blk, :] = kop_ref[...]
    ko_ref[blk:, :] = koc_ref[...]
    v_ref[0:blk, :] = vp_ref[...]
    v_ref[blk:, :] = vc_ref[...]

    row = lax.broadcasted_iota(jnp.int32, (blk, blk), 0)
    col = lax.broadcasted_iota(jnp.int32, (blk, blk), 1)
    lower = col <= row
    no_prev = (col - row) > jnp.where(i > 0, blk, 0)
    left = _left_half()
    for b0 in range(0, tq, blk):
        for kv in range(ATTN_KV_HEADS):
            lanes = slice(kv * LANES, (kv + 1) * LANES)
            qt = jnp.concatenate(
                [q_ref[b0:b0 + blk, (2 * kv) * LANES:(2 * kv + 1) * LANES],
                 q_ref[b0:b0 + blk, (2 * kv + 1) * LANES:(2 * kv + 2) * LANES]], axis=0)
            s_even = _dot_nt(qt, ke_ref[b0:b0 + 2 * blk, lanes])
            s_odd = _dot_nt(qt, ko_ref[b0:b0 + 2 * blk, lanes])
            p_cur, p_prev, r_den = [], [], []
            for g in range(group):
                src = s_even if g % 2 == 0 else s_odd
                r0 = (g // 2) * blk
                hh = kv * group + g
                s = jnp.where(lower, src[r0:r0 + blk, blk:], src[r0:r0 + blk, :blk]) + bias_ref[hh]
                if b0 == 0:
                    s = jnp.where(no_prev, -jnp.inf, s)
                sink = sink_ref[hh]
                m = jnp.maximum(jnp.max(s, axis=-1, keepdims=True), sink)
                e = jnp.exp(s - m)
                r_den.append(1.0 / (jnp.sum(e, axis=-1, keepdims=True) + jnp.exp(sink - m)))
                p_cur.append(jnp.where(lower, e, 0.0).astype(BF16))
                p_prev.append(jnp.where(lower, 0.0, e).astype(BF16))
            o4 = (_dot(jnp.concatenate(p_cur, axis=0), v_ref[b0 + blk:b0 + 2 * blk, lanes])
                  + _dot(jnp.concatenate(p_prev, axis=0), v_ref[b0:b0 + blk, lanes]))
            for pair in range(group // 2):
                ga, gb = 2 * pair, 2 * pair + 1
                oa = o4[ga * blk:(ga + 1) * blk, :] * r_den[ga]
                ob = o4[gb * blk:(gb + 1) * blk, :] * r_den[gb]
                c0 = (2 * kv + pair) * LANES
                oacc_ref[b0:b0 + blk, c0:c0 + LANES] = jnp.where(left, oa, ob).astype(BF16)
    o_ref[...] = h_ref[...] + _dot(oacc_ref[...], wo_ref[...])


def _alibi_bias():
    blk = ATTN_BLOCK
    row = jnp.arange(blk, dtype=jnp.int32)[:, None]
    col = jnp.arange(blk, dtype=jnp.int32)[None, :]
    rel = jnp.where(col <= row, row - col, row - col + blk).astype(F32)
    slopes = jnp.exp2(-8.0 * (jnp.arange(ATTN_HEADS, dtype=F32) + 1.0) / ATTN_HEADS)
    return -(slopes[:, None, None] * rel[None])


def _attention_layer(h, batch, seq, g, wqkv, gq, gk, sinks, wo, *, tm=512):
    t, d = h.shape
    hd, blk = ATTN_HEAD_DIM, ATTN_BLOCK
    nq, nkp = ATTN_HEADS * hd, ATTN_KV_HEADS * LANES
    tm = min(tm, seq)
    row = lambda i: (i, 0)
    kv_shape = jax.ShapeDtypeStruct((t, nkp), BF16)
    q, ke, ko, v2 = pl.pallas_call(
        _qkv_kernel,
        grid=(t // tm,),
        in_specs=[pl.BlockSpec((tm, d), row), _const_spec(g.shape), _const_spec(wqkv.shape),
                  _const_spec(gq.shape), _const_spec(gk.shape)],
        out_specs=[pl.BlockSpec((tm, nq), row)] + [pl.BlockSpec((tm, nkp), row)] * 3,
        out_shape=[jax.ShapeDtypeStruct((t, nq), BF16), kv_shape, kv_shape, kv_shape],
        compiler_params=_params(1),
        name="attn_qkv",
    )(h, g, wqkv, gq, gk)

    bias = _alibi_bias()
    nt = seq // tm
    per = tm // blk
    cur = lambda b, i: (b * nt + i, 0)
    prev = lambda b, i: (jnp.maximum((b * nt + i) * per - 1, 0), 0)
    kv_cur, kv_prev = pl.BlockSpec((tm, nkp), cur), pl.BlockSpec((blk, nkp), prev)
    return pl.pallas_call(
        _attn_kernel,
        grid=(batch, nt),
        in_specs=[pl.BlockSpec(memory_space=pltpu.SMEM), _const_spec(bias.shape),
                  pl.BlockSpec((tm, d), cur), pl.BlockSpec((tm, nq), cur),
                  kv_cur, kv_prev, kv_cur, kv_prev, kv_cur, kv_prev,
                  _const_spec(wo.shape)],
        out_specs=pl.BlockSpec((tm, d), cur),
        out_shape=jax.ShapeDtypeStruct((t, d), F32),
        scratch_shapes=[pltpu.VMEM((tm + blk, nkp), BF16), pltpu.VMEM((tm + blk, nkp), BF16),
                        pltpu.VMEM((tm + blk, nkp), BF16), pltpu.VMEM((tm, nq), BF16)],
        compiler_params=_params(2),
        name="attn_core",
    )(sinks, bias, h, q, ke, ke, ko, ko, v2, v2, wo)


CONV_HALO = 32


def _conformer_kernel(h_ref, g_ref, w1_ref, b1_ref, wdw_ref, bdw_ref, lg_ref, lb_ref, w2_ref, b2_ref,
                      o_ref, ycat_ref, ysh_ref, conv_ref, *, row_chunk, col_chunk):
    i = pl.program_id(1)
    n_streams, tm, d = h_ref.shape
    halo = CONV_HALO
    span = tm + halo - SUBLANES
    first = halo - (CONV_WIDTH - 1)

    @pl.when(i == 0)
    def _():
        ycat_ref[:, 0:halo, :] = jnp.zeros((n_streams, halo, d), F32)

    for s in range(n_streams):
        u = _rms(h_ref[s], g_ref[...]).astype(BF16)
        a = _dot(u, w1_ref[...]) + b1_ref[...]
        ycat_ref[s, halo:, :] = a[:, :d] * jax.nn.sigmoid(a[:, d:])

    for s in range(n_streams):
        for ph in range(1, SUBLANES):
            ysh_ref[s, ph - 1] = ycat_ref[s, ph:ph + span, :]
        for r0 in range(0, tm, row_chunk):
            for c0 in range(0, d, col_chunk):
                cols = slice(c0, c0 + col_chunk)
                acc = jnp.zeros((row_chunk, col_chunk), F32)
                for k in range(CONV_WIDTH):
                    phase = (first + k) % SUBLANES
                    base = r0 + first + k - phase
                    if phase == 0:
                        tap = ycat_ref[s, base:base + row_chunk, cols]
                    else:
                        tap = ysh_ref[s, phase - 1, base:base + row_chunk, cols]
                    acc = acc + wdw_ref[k:k + 1, cols] * tap
                conv_ref[s, r0:r0 + row_chunk, cols] = acc + bdw_ref[:, cols]
        ycat_ref[s, 0:halo, :] = ycat_ref[s, tm:tm + halo, :]

    for s in range(n_streams):
        c = conv_ref[s]
        mu = jnp.mean(c, axis=-1, keepdims=True)
        cc = c - mu
        var = jnp.mean(cc * cc, axis=-1, keepdims=True)
        y = cc * lax.rsqrt(var + EPS) * lg_ref[...] + lb_ref[...]
        y = y * jax.nn.sigmoid(y)
        o_ref[s] = h_ref[s] + _dot(y.astype(BF16), w2_ref[...]) + b2_ref[...]


def _conformer_layer(h, batch, seq, g, w1, b1, wdw, bdw, lg, lb, w2, b2, *, tm=256, n_streams=2):
    t, d = h.shape
    tm = min(tm, seq)
    n_streams = min(n_streams, batch)
    cur = lambda b, i: (b, i, 0)
    out = pl.pallas_call(
        functools.partial(_conformer_kernel, row_chunk=64, col_chunk=256),
        grid=(batch // n_streams, seq // tm),
        in_specs=[pl.BlockSpec((n_streams, tm, d), cur), _const_spec(g.shape), _const_spec(w1.shape),
                  _const_spec(b1.shape), _const_spec(wdw.shape), _const_spec(bdw.shape),
                  _const_spec(lg.shape), _const_spec(lb.shape), _const_spec(w2.shape),
                  _const_spec(b2.shape)],
        out_specs=pl.BlockSpec((n_streams, tm, d), cur),
        out_shape=jax.ShapeDtypeStruct((batch, seq, d), F32),
        scratch_shapes=[pltpu.VMEM((n_streams, tm + CONV_HALO, d), F32),
                        pltpu.VMEM((n_streams, SUBLANES - 1, tm + CONV_HALO - SUBLANES, d), F32),
                        pltpu.VMEM((n_streams, tm, d), F32)],
        compiler_params=_params(2),
        name="conformer",
    )(h.reshape(batch, seq, d), g, w1, b1, wdw, bdw, lg, lb, w2, b2)
    return out.reshape(t, d)


def _mamba_in_kernel(h_ref, g_ref, win_ref, wc_ref, bc_ref, dtb_ref, alog_ref,
                     expand_ref, z_ref, xs_ref, xdt_ref, bt_ref, c_ref, cs_ref, cst_ref,
                     xcat_ref, bact_ref, dte_ref, *, proj_chunk, row_chunk, col_chunk):
    i = pl.program_id(1)
    tm = h_ref.shape[0]
    di = xs_ref.shape[1]
    gn = c_ref.shape[1]
    L = SSM_CHUNK
    halo = SUBLANES
    n_xbc = di + 2 * gn
    u = _rms(h_ref[...], g_ref[...]).astype(BF16)
    dt = jax.nn.softplus(_dot(u, win_ref[:, di + n_xbc:]) + dtb_ref[...])

    @pl.when(i == 0)
    def _():
        xcat_ref[:, 0:halo, :] = jnp.zeros((xcat_ref.shape[0], halo, proj_chunk), F32)

    def project_z(q):
        z_ref[:, q * proj_chunk:(q + 1) * proj_chunk] = _dot(u, win_ref[:, q * proj_chunk:(q + 1) * proj_chunk])

    def project_xbc(panel):
        p0 = di + panel * proj_chunk
        xcat_ref[panel, halo:, :] = _dot(u, win_ref[:, p0:p0 + proj_chunk])

    def time_sums():
        r_i = lax.broadcasted_iota(jnp.int32, (L, L), 0)
        c_i = lax.broadcasted_iota(jnp.int32, (L, L), 1)
        tri = (r_i >= c_i).astype(F32)
        a_row = -jnp.exp(alog_ref[...])
        for c in range(tm // L):
            cs = jnp.dot(tri, dt[c * L:(c + 1) * L, :] * a_row, precision=lax.Precision.HIGHEST,
                         preferred_element_type=F32)
            cs_ref[c * L:(c + 1) * L, :] = cs
            cst_ref[c] = cs.T

    def expand_dt():
        dt_hi = dt.astype(BF16)
        rem = dt - dt_hi.astype(F32)
        dt_mid = rem.astype(BF16)
        dt_lo = (rem - dt_mid.astype(F32)).astype(BF16)
        dte_ref[...] = _dot(jnp.concatenate([dt_hi, dt_mid, dt_lo], axis=1), expand_ref[...])

    def conv_silu(panel):
        p0 = panel * proj_chunk
        for r0 in range(0, tm, row_chunk):
            for c0 in range(p0, p0 + proj_chunk, col_chunk):
                rows = slice(r0, r0 + row_chunk)
                cols = slice(c0, c0 + col_chunk)
                xb = xcat_ref[panel, r0:r0 + halo + row_chunk, c0 - p0:c0 - p0 + col_chunk]
                x1 = pltpu.roll(xb, 1, axis=0)
                near = wc_ref[3:4, cols] * xb + wc_ref[2:3, cols] * x1
                far = wc_ref[1:2, cols] * xb + wc_ref[0:1, cols] * x1
                acc = near[halo:, :] + pltpu.roll(far, 2, axis=0)[halo:, :] + bc_ref[:, cols]
                act = acc * jax.nn.sigmoid(acc)
                if c0 < di:
                    xs_ref[rows, cols] = act
                    xdt_ref[rows, cols] = (act * dte_ref[rows, cols]).astype(BF16)
                elif c0 < di + gn:
                    bact_ref[rows, c0 - di:c0 - di + col_chunk] = act
                else:
                    c_ref[rows, c0 - di - gn:c0 - di - gn + col_chunk] = act.astype(BF16)

    def transpose_b():
        for c in range(tm // L):
            for g in range(SSM_GROUPS):
                n0 = g * SSM_STATE
                bt_ref[c, n0:n0 + SSM_STATE, :] = bact_ref[c * L:(c + 1) * L, n0:n0 + SSM_STATE].T.astype(BF16)

    n_x, n_b = di // proj_chunk, gn // proj_chunk
    order = list(range(n_x, n_xbc // proj_chunk)) + list(range(n_x))
    project_xbc(order[0])
    for step, panel in enumerate(order):
        if step + 1 < len(order):
            project_xbc(order[step + 1])
        if step < n_x:
            project_z(step)
        if step == 0:
            time_sums()
        if step == 1:
            expand_dt()
        conv_silu(panel)
        if panel == n_x + n_b - 1:
            transpose_b()
    xcat_ref[:, 0:halo, :] = xcat_ref[:, tm:tm + halo, :]


def _ssd_kernel(h_ref, z_ref, xs_ref, xdt_ref, bt_ref, c_ref, cs_ref, cst_ref, dexp_ref, ng_ref,
                wout_ref, o_ref, state_ref, y_ref):
    i = pl.program_id(1)
    tm = h_ref.shape[0]
    L = SSM_CHUNK
    gw = 4 * SSM_HEAD_DIM

    @pl.when(i == 0)
    def _():
        state_ref[...] = jnp.zeros(state_ref.shape, F32)

    r_i = lax.broadcasted_iota(jnp.int32, (L, L), 0)
    c_i = lax.broadcasted_iota(jnp.int32, (L, L), 1)
    causal = r_i >= c_i
    left = _left_half()

    def chunk(c, carry):
        r0 = pl.multiple_of(c * L, L)
        cs = cs_ref[pl.ds(r0, L), :]
        cs_t = cst_ref[c]
        for g in range(SSM_GROUPS):
            n0 = g * SSM_STATE
            bg_t = bt_ref[c, n0:n0 + SSM_STATE, :]
            cg = c_ref[pl.ds(r0, L), n0:n0 + SSM_STATE]
            cb = _dot(cg, bg_t)
            xg = xdt_ref[pl.ds(r0, L), g * gw:(g + 1) * gw].astype(F32)
            y_tiles, ecs_tiles, dec_tiles = [], [], []
            for pair in range(2):
                xt = xg[:, pair * LANES:(pair + 1) * LANES]
                ms, bcs = [], []
                for half in range(2):
                    hh = 4 * g + 2 * pair + half
                    bc = jnp.broadcast_to(cs[:, hh:hh + 1], (L, L))
                    lm = jnp.exp(jnp.where(causal, bc - cs_t[hh:hh + 1, :], -jnp.inf))
                    ms.append((cb * lm).astype(BF16))
                    bcs.append(bc)
                x_stack = jnp.concatenate([jnp.where(left, xt, 0.0), jnp.where(left, 0.0, xt)],
                                          axis=0).astype(BF16)
                y_tiles.append(_dot(jnp.concatenate(ms, axis=1), x_stack))
                cs_e = jnp.where(left, bcs[0], bcs[1])
                ecs_tiles.append(jnp.exp(cs_e))
                dec_tiles.append(jnp.exp(cs_e[L - 1:L, :] - cs_e))
            ecs_e = jnp.concatenate(ecs_tiles, axis=1)
            st = state_ref[g]
            y_off = _dot(cg, st.astype(BF16)) * ecs_e
            y_ref[pl.ds(r0, L), g * gw:(g + 1) * gw] = jnp.concatenate(y_tiles, axis=1) + y_off
            xw = (xg * jnp.concatenate(dec_tiles, axis=1)).astype(BF16)
            state_ref[g] = st * ecs_e[L - 1:L, :] + _dot(bg_t, xw)
        return carry

    lax.fori_loop(0, tm // L, chunk, 0)

    y = y_ref[...] + xs_ref[...] * dexp_ref[...]
    zz = z_ref[...]
    y = y * (zz * jax.nn.sigmoid(zz))
    o_ref[...] = h_ref[...] + _dot(_rms(y, ng_ref[...]).astype(BF16), wout_ref[...])


def _mamba_layer(h, batch, seq, g, win, wc, bc, dtb, alog, dexp, ng, wout, *,
                 tm_in=256, tm_ssd=512, proj_chunk=512):
    t, d = h.shape
    di = wout.shape[0]
    gn = SSM_GROUPS * SSM_STATE
    L = SSM_CHUNK
    tm = min(tm_in, seq)
    nt = seq // tm
    cur = lambda b, i: (b * nt + i, 0)
    cur3 = lambda b, i: (b * nt + i, 0, 0)
    expand = jnp.tile(jnp.repeat(jnp.eye(LANES, di // SSM_HEAD_DIM, dtype=BF16), SSM_HEAD_DIM, axis=1),
                      (3, 1))
    z, xs, xdt, bt, cm, cs, cst = pl.pallas_call(
        functools.partial(_mamba_in_kernel, proj_chunk=proj_chunk, row_chunk=64, col_chunk=256),
        grid=(batch, nt),
        in_specs=[pl.BlockSpec((tm, d), cur), _const_spec(g.shape), _const_spec(win.shape),
                  _const_spec(wc.shape), _const_spec(bc.shape), _const_spec(dtb.shape),
                  _const_spec(alog.shape), _const_spec(expand.shape)],
        out_specs=[pl.BlockSpec((tm, di), cur), pl.BlockSpec((tm, di), cur),
                   pl.BlockSpec((tm, di), cur), pl.BlockSpec((tm // L, gn, L), cur3),
                   pl.BlockSpec((tm, gn), cur), pl.BlockSpec((tm, LANES), cur),
                   pl.BlockSpec((tm // L, LANES, L), cur3)],
        out_shape=[jax.ShapeDtypeStruct((t, di), F32), jax.ShapeDtypeStruct((t, di), F32),
                   jax.ShapeDtypeStruct((t, di), BF16), jax.ShapeDtypeStruct((t // L, gn, L), BF16),
                   jax.ShapeDtypeStruct((t, gn), BF16), jax.ShapeDtypeStruct((t, LANES), F32),
                   jax.ShapeDtypeStruct((t // L, LANES, L), F32)],
        scratch_shapes=[pltpu.VMEM(((di + 2 * gn) // proj_chunk, tm + SUBLANES, proj_chunk), F32),
                        pltpu.VMEM((tm, gn), F32),
                        pltpu.VMEM((tm, di), F32)],
        compiler_params=_params(2),
        name="mamba_in",
    )(h, g, win, wc, bc, dtb, alog, expand)

    tm = min(tm_ssd, seq)
    nt = seq // tm
    cur = lambda b, i: (b * nt + i, 0)
    cur3 = lambda b, i: (b * nt + i, 0, 0)
    return pl.pallas_call(
        _ssd_kernel,
        grid=(batch, nt),
        in_specs=[pl.BlockSpec((tm, d), cur), pl.BlockSpec((tm, di), cur),
                  pl.BlockSpec((tm, di), cur), pl.BlockSpec((tm, di), cur),
                  pl.BlockSpec((tm // L, gn, L), cur3), pl.BlockSpec((tm, gn), cur),
                  pl.BlockSpec((tm, LANES), cur), pl.BlockSpec((tm // L, LANES, L), cur3),
                  _const_spec(dexp.shape), _const_spec(ng.shape), _const_spec(wout.shape)],
        out_specs=pl.BlockSpec((tm, d), cur),
        out_shape=jax.ShapeDtypeStruct((t, d), F32),
        scratch_shapes=[pltpu.VMEM((SSM_GROUPS, SSM_STATE, 4 * SSM_HEAD_DIM), F32),
                        pltpu.VMEM((tm, di), F32)],
        compiler_params=_params(2),
        name="mamba_ssd",
    )(h, z, xs, xdt, bt, cm, cs, cst, dexp, ng, wout)


def _row(v):
    return v.reshape(1, -1).astype(F32)


def _pad_cols(a, n):
    return jnp.pad(a, ((0, 0), (0, n - a.shape[1])))


def kernel(x, p, mix_norm_g, mlp_norm_g, ple_norm_g, a_wqkv, a_q_norm_g, a_k_norm_g, a_sinks, a_wo, b_w_pw1, b_b_pw1, b_w_dw, b_b_dw, b_ln_g, b_ln_b, b_w_pw2, b_b_pw2, c_w_in, c_w_conv, c_b_conv, c_dt_bias, c_A_log, c_D, c_norm_g, c_w_out, m_w1, m_w2, ple_w_proj, ple_w_gate):
    batch, seq, d = x.shape
    depth = p.shape[0]
    t = batch * seq
    h = x.reshape(t, d)
    p = p.reshape(depth, t, -1)
    for i in range(depth):
        kind, j = i % N_MIXERS, i // N_MIXERS
        g = _row(mix_norm_g[i])
        if kind == 0:
            scale = ATTN_HEAD_DIM ** -0.5
            gq = _row(jnp.tile(a_q_norm_g[j] * scale, 2))
            gk = _row(jnp.tile(a_k_norm_g[j], 2))
            h = _attention_layer(h, batch, seq, g, a_wqkv[j].astype(BF16), gq, gk,
                                 a_sinks[j].astype(F32), a_wo[j].astype(BF16))
        elif kind == 1:
            wdw = jnp.pad(b_w_dw[j], ((0, CONV_HALO - CONV_WIDTH), (0, 0)))
            h = _conformer_layer(h, batch, seq, g, b_w_pw1[j].astype(BF16), _row(b_b_pw1[j]),
                                 wdw, _row(b_b_dw[j]), _row(b_ln_g[j]), _row(b_ln_b[j]),
                                 b_w_pw2[j].astype(BF16), _row(b_b_pw2[j]))
        else:
            di = c_w_out.shape[1]
            gn = SSM_GROUPS * SSM_STATE
            win = _pad_cols(c_w_in[j], 2 * di + 2 * gn + LANES).astype(BF16)
            wc = jnp.pad(c_w_conv[j], ((0, SUBLANES - SSM_CONV), (0, 0)))
            dtb = _pad_cols(_row(c_dt_bias[j]), LANES)
            alog = _pad_cols(_row(c_A_log[j]), LANES)
            dexp = _row(jnp.repeat(c_D[j], SSM_HEAD_DIM))
            h = _mamba_layer(h, batch, seq, g, win, wc, _row(c_b_conv[j]), dtb, alog,
                             dexp, _row(c_norm_g[j]), c_w_out[j].astype(BF16))
        h = _mlp_ple(h, p, i, _row(mlp_norm_g[i]), _row(ple_norm_g[i]),
                     m_w1[i].astype(BF16), m_w2[i].astype(BF16),
                     ple_w_gate[i].astype(BF16), ple_w_proj[i].astype(BF16))
    return h.reshape(batch, seq, d)
```

```python
import functools

import jax
import jax.numpy as jnp
from jax import lax
from jax.experimental import pallas as pl
from jax.experimental.pallas import tpu as pltpu

F32 = jnp.float32
BF16 = jnp.bfloat16
EPS = 1e-6

N_MIXERS = 3
ATTN_HEADS = 16
ATTN_KV_HEADS = 4
ATTN_HEAD_DIM = 64
ATTN_BLOCK = 128
CONV_WIDTH = 31
SSM_HEAD_DIM = 64
SSM_GROUPS = 8
SSM_STATE = 128
SSM_CONV = 4
SSM_CHUNK = 128

LANES = 128
SUBLANES = 8
VMEM_LIMIT_BYTES = 56 * 1024 * 1024


def _params(n_grid_dims):
    return pltpu.CompilerParams(
        dimension_semantics=("arbitrary",) * n_grid_dims,
        vmem_limit_bytes=VMEM_LIMIT_BYTES)


def _const_spec(shape):
    nd = len(shape)
    return pl.BlockSpec(shape, lambda *_: (0,) * nd, pipeline_mode=pl.Buffered(1))


def _rms(x, g):
    return x * lax.rsqrt(jnp.mean(x * x, axis=-1, keepdims=True) + EPS) * g


def _dot(a, b):
    return jnp.dot(a, b, preferred_element_type=F32)


def _dot_nt(a, b):
    return lax.dot_general(a, b, (((1,), (1,)), ((), ())), preferred_element_type=F32)


def _left_half():
    return lax.broadcasted_iota(jnp.int32, (1, LANES), 1) < (LANES // 2)


def _mlp_ple_kernel(h_ref, p_ref, g1_ref, g2_ref, w1_ref, w2_ref, wg_ref, wp_ref, o_ref, *,
                    hidden_chunk):
    x = h_ref[...]
    u = _rms(x, g1_ref[...]).astype(BF16)
    acc = jnp.zeros_like(x)
    for c in range(0, w1_ref.shape[1], hidden_chunk):
        a = _dot(u, w1_ref[:, c:c + hidden_chunk])
        a = jnp.square(jnp.maximum(a, 0.0)).astype(BF16)
        acc = acc + _dot(a, w2_ref[c:c + hidden_chunk, :])
    h2 = x + acc
    u2 = _rms(h2, g2_ref[...]).astype(BF16)
    gate = jax.nn.sigmoid(_dot(u2, wg_ref[...]))
    proj = _dot(p_ref[...].astype(BF16), wp_ref[...])
    o_ref[...] = h2 + gate * proj


def _layer_spec(stack, layer):
    return pl.BlockSpec((None,) + stack.shape[1:], lambda *_: (layer, 0, 0), pipeline_mode=pl.Buffered(1))


def _mlp_ple(h, p, layer, g1, g2, w1, w2, wg, wp, *, tm=1024, hidden_chunk=512):
    t, d = h.shape
    tm = min(tm, t)
    row = lambda i: (i, 0)
    return pl.pallas_call(
        functools.partial(_mlp_ple_kernel, hidden_chunk=hidden_chunk),
        grid=(t // tm,),
        in_specs=[pl.BlockSpec((tm, d), row),
                  pl.BlockSpec((None, tm, p.shape[2]), lambda i: (layer, i, 0)),
                  _const_spec(g1.shape), _const_spec(g2.shape), _layer_spec(w1, layer),
                  _layer_spec(w2, layer), _layer_spec(wg, layer), _layer_spec(wp, layer)],
        out_specs=pl.BlockSpec((tm, d), row),
        out_shape=jax.ShapeDtypeStruct((t, d), F32),
        compiler_params=_params(1),
        name="mlp_ple",
    )(h, p, g1, g2, w1, w2, wg, wp)


def _qkv_kernel(h_ref, g_ref, w_ref, gq_ref, gk_ref, seg_ref, q_ref, ke_ref, ko_ref, v2_ref, *, panel):
    hd = ATTN_HEAD_DIM
    nq = q_ref.shape[1]
    nk = ATTN_KV_HEADS * hd
    u = _rms(h_ref[...], g_ref[...]).astype(BF16)
    left = _left_half()

    def pair_norm(t, g):
        sq = t * t
        sq_hi = sq.astype(BF16)
        sq_lo = (sq - sq_hi.astype(F32)).astype(BF16)
        ssq = _dot(jnp.concatenate([sq_hi, sq_lo], axis=1), seg_ref[...])
        return t * lax.rsqrt(ssq * (1.0 / hd) + EPS) * g

    def project(c0):
        return _dot(u, w_ref[:, c0:c0 + panel])

    def finish(c0, acc):
        for t0 in range(0, panel, LANES):
            c = c0 + t0
            t = acc[:, t0:t0 + LANES]
            if c < nq:
                q_ref[:, c:c + LANES] = pair_norm(t, gq_ref[...]).astype(BF16)
                continue
            j = (c - nq) % nk // LANES
            first, second = 2 * j * LANES, (2 * j + 1) * LANES
            if c < nq + nk:
                t = pair_norm(t, gk_ref[...])
                r = pltpu.roll(t, hd, axis=1)
                zero = jnp.zeros_like(t)
                ke_ref[:, first:first + LANES] = jnp.where(left, t, zero).astype(BF16)
                ke_ref[:, second:second + LANES] = jnp.where(left, r, zero).astype(BF16)
                ko_ref[:, first:first + LANES] = jnp.where(left, zero, r).astype(BF16)
                ko_ref[:, second:second + LANES] = jnp.where(left, zero, t).astype(BF16)
            else:
                r = pltpu.roll(t, hd, axis=1)
                v2_ref[:, first:first + LANES] = jnp.where(left, t, r).astype(BF16)
                v2_ref[:, second:second + LANES] = jnp.where(left, r, t).astype(BF16)

    starts = list(range(0, w_ref.shape[1], panel))
    acc = project(starts[0])
    for idx, c0 in enumerate(starts):
        nxt = project(starts[idx + 1]) if idx + 1 < len(starts) else None
        finish(c0, acc)
        acc = nxt


def _attn_kernel(sink_ref, bias_ref, h_ref, q_ref, kec_ref, kep_ref, koc_ref, kop_ref, vc_ref, vp_ref,
                 wo_ref, o_ref, ke_ref, ko_ref, v_ref, oacc_ref):
    blk = ATTN_BLOCK
    group = ATTN_HEADS // ATTN_KV_HEADS
    i = pl.program_id(1)
    tq = q_ref.shape[0]
    ke_ref[0:blk, :] = kep_ref[...]
    ke_ref[blk:, :] = kec_ref[...]
    ko_ref[0:blk, :] = kop_ref[...]
    ko_ref[blk:, :] = koc_ref[...]
    v_ref[0:blk, :] = vp_ref[...]
    v_ref[blk:, :] = vc_ref[...]

    row = lax.broadcasted_iota(jnp.int32, (blk, blk), 0)
    col = lax.broadcasted_iota(jnp.int32, (blk, blk), 1)
    lower = col <= row
    no_prev = (col - row) > jnp.where(i > 0, blk, 0)
    left = _left_half()
    for b0 in range(0, tq, blk):
        for kv in range(ATTN_KV_HEADS):
            lanes = slice(kv * LANES, (kv + 1) * LANES)
            qt = jnp.concatenate(
                [q_ref[b0:b0 + blk, (2 * kv) * LANES:(2 * kv + 1) * LANES],
                 q_ref[b0:b0 + blk, (2 * kv + 1) * LANES:(2 * kv + 2) * LANES]], axis=0)
            s_even = _dot_nt(qt, ke_ref[b0:b0 + 2 * blk, lanes])
            s_odd = _dot_nt(qt, ko_ref[b0:b0 + 2 * blk, lanes])
            p_cur, p_prev, r_den = [], [], []
            for g in range(group):
                src = s_even if g % 2 == 0 else s_odd
                r0 = (g // 2) * blk
                hh = kv * group + g
                s = jnp.where(lower, src[r0:r0 + blk, blk:], src[r0:r0 + blk, :blk]) + bias_ref[hh]
                if b0 == 0:
                    s = jnp.where(no_prev, -jnp.inf, s)
                sink = sink_ref[hh]
                m = jnp.maximum(jnp.max(s, axis=-1, keepdims=True), sink)
                e = jnp.exp(s - m)
                r_den.append(1.0 / (jnp.sum(e, axis=-1, keepdims=True) + jnp.exp(sink - m)))
                p_cur.append(jnp.where(lower, e, 0.0).astype(BF16))
                p_prev.append(jnp.where(lower, 0.0, e).astype(BF16))
            o4 = (_dot(jnp.concatenate(p_cur, axis=0), v_ref[b0 + blk:b0 + 2 * blk, lanes])
                  + _dot(jnp.concatenate(p_prev, axis=0), v_ref[b0:b0 + blk, lanes]))
            for pair in range(group // 2):
                ga, gb = 2 * pair, 2 * pair + 1
                oa = o4[ga * blk:(ga + 1) * blk, :] * r_den[ga]
                ob = o4[gb * blk:(gb + 1) * blk, :] * r_den[gb]
                c0 = (2 * kv + pair) * LANES
                oacc_ref[b0:b0 + blk, c0:c0 + LANES] = jnp.where(left, oa, ob).astype(BF16)
    o_ref[...] = h_ref[...] + _dot(oacc_ref[...], wo_ref[...])


def _alibi_bias():
    blk = ATTN_BLOCK
    row = jnp.arange(blk, dtype=jnp.int32)[:, None]
    col = jnp.arange(blk, dtype=jnp.int32)[None, :]
    rel = jnp.where(col <= row, row - col, row - col + blk).astype(F32)
    slopes = jnp.exp2(-8.0 * (jnp.arange(ATTN_HEADS, dtype=F32) + 1.0) / ATTN_HEADS)
    return -(slopes[:, None, None] * rel[None])


def _attention_layer(h, batch, seq, g, wqkv, gq, gk, sinks, wo, *, tm=512):
    t, d = h.shape
    hd, blk = ATTN_HEAD_DIM, ATTN_BLOCK
    nq, nkp = ATTN_HEADS * hd, ATTN_KV_HEADS * LANES
    tm = min(tm, seq)
    row = lambda i: (i, 0)
    kv_shape = jax.ShapeDtypeStruct((t, nkp), BF16)
    lane_head = jnp.arange(LANES) // hd
    seg = jnp.tile((lane_head[:, None] == lane_head[None, :]).astype(BF16), (2, 1))
    q, ke, ko, v2 = pl.pallas_call(
        functools.partial(_qkv_kernel, panel=2 * LANES),
        grid=(t // tm,),
        in_specs=[pl.BlockSpec((tm, d), row), _const_spec(g.shape), _const_spec(wqkv.shape),
                  _const_spec(gq.shape), _const_spec(gk.shape), _const_spec(seg.shape)],
        out_specs=[pl.BlockSpec((tm, nq), row)] + [pl.BlockSpec((tm, nkp), row)] * 3,
        out_shape=[jax.ShapeDtypeStruct((t, nq), BF16), kv_shape, kv_shape, kv_shape],
        compiler_params=_params(1),
        name="attn_qkv",
    )(h, g, wqkv, gq, gk, seg)

    bias = _alibi_bias()
    nt = seq // tm
    per = tm // blk
    cur = lambda b, i: (b * nt + i, 0)
    prev = lambda b, i: (jnp.maximum((b * nt + i) * per - 1, 0), 0)
    kv_cur, kv_prev = pl.BlockSpec((tm, nkp), cur), pl.BlockSpec((blk, nkp), prev)
    return pl.pallas_call(
        _attn_kernel,
        grid=(batch, nt),
        in_specs=[pl.BlockSpec(memory_space=pltpu.SMEM), _const_spec(bias.shape),
                  pl.BlockSpec((tm, d), cur), pl.BlockSpec((tm, nq), cur),
                  kv_cur, kv_prev, kv_cur, kv_prev, kv_cur, kv_prev,
                  _const_spec(wo.shape)],
        out_specs=pl.BlockSpec((tm, d), cur),
        out_shape=jax.ShapeDtypeStruct((t, d), F32),
        scratch_shapes=[pltpu.VMEM((tm + blk, nkp), BF16), pltpu.VMEM((tm + blk, nkp), BF16),
                        pltpu.VMEM((tm + blk, nkp), BF16), pltpu.VMEM((tm, nq), BF16)],
        compiler_params=_params(2),
        name="attn_core",
    )(sinks, bias, h, q, ke, ke, ko, ko, v2, v2, wo)


CONV_HALO = 32


def _conformer_kernel(h_ref, g_ref, w1_ref, b1_ref, wdw_ref, bdw_ref, lg_ref, lb_ref, w2_ref, b2_ref,
                      o_ref, ycat_ref, ysh_ref, conv_ref, *, row_chunk, col_chunk):
    i = pl.program_id(1)
    n_streams, tm, d = h_ref.shape
    halo = CONV_HALO
    span = tm + halo - SUBLANES
    first = halo - (CONV_WIDTH - 1)

    @pl.when(i == 0)
    def _():
        ycat_ref[:, 0:halo, :] = jnp.zeros((n_streams, halo, d), F32)

    for s in range(n_streams):
        u = _rms(h_ref[s], g_ref[...]).astype(BF16)
        a = _dot(u, w1_ref[...]) + b1_ref[...]
        ycat_ref[s, halo:, :] = a[:, :d] * jax.nn.sigmoid(a[:, d:])

    for s in range(n_streams):
        for ph in range(1, SUBLANES):
            ysh_ref[s, ph - 1] = ycat_ref[s, ph:ph + span, :]
        for r0 in range(0, tm, row_chunk):
            for c0 in range(0, d, col_chunk):
                cols = slice(c0, c0 + col_chunk)
                acc = jnp.zeros((row_chunk, col_chunk), F32)
                for k in range(CONV_WIDTH):
                    phase = (first + k) % SUBLANES
                    base = r0 + first + k - phase
                    if phase == 0:
                        tap = ycat_ref[s, base:base + row_chunk, cols]
                    else:
                        tap = ysh_ref[s, phase - 1, base:base + row_chunk, cols]
                    acc = acc + wdw_ref[k:k + 1, cols] * tap
                conv_ref[s, r0:r0 + row_chunk, cols] = acc + bdw_ref[:, cols]
        ycat_ref[s, 0:halo, :] = ycat_ref[s, tm:tm + halo, :]

    for s in range(n_streams):
        c = conv_ref[s]
        mu = jnp.mean(c, axis=-1, keepdims=True)
        cc = c - mu
        var = jnp.mean(cc * cc, axis=-1, keepdims=True)
        y = cc * lax.rsqrt(var + EPS) * lg_ref[...] + lb_ref[...]
        y = y * jax.nn.sigmoid(y)
        o_ref[s] = h_ref[s] + _dot(y.astype(BF16), w2_ref[...]) + b2_ref[...]


def _conformer_layer(h, batch, seq, g, w1, b1, wdw, bdw, lg, lb, w2, b2, *, tm=256, n_streams=2):
    t, d = h.shape
    tm = min(tm, seq)
    n_streams = min(n_streams, batch)
    cur = lambda b, i: (b, i, 0)
    out = pl.pallas_call(
        functools.partial(_conformer_kernel, row_chunk=64, col_chunk=256),
        grid=(batch // n_streams, seq // tm),
        in_specs=[pl.BlockSpec((n_streams, tm, d), cur), _const_spec(g.shape), _const_spec(w1.shape),
                  _const_spec(b1.shape), _const_spec(wdw.shape), _const_spec(bdw.shape),
                  _const_spec(lg.shape), _const_spec(lb.shape), _const_spec(w2.shape),
                  _const_spec(b2.shape)],
        out_specs=pl.BlockSpec((n_streams, tm, d), cur),
        out_shape=jax.ShapeDtypeStruct((batch, seq, d), F32),
        scratch_shapes=[pltpu.VMEM((n_streams, tm + CONV_HALO, d), F32),
                        pltpu.VMEM((n_streams, SUBLANES - 1, tm + CONV_HALO - SUBLANES, d), F32),
                        pltpu.VMEM((n_streams, tm, d), F32)],
        compiler_params=_params(2),
        name="conformer",
    )(h.reshape(batch, seq, d), g, w1, b1, wdw, bdw, lg, lb, w2, b2)
    return out.reshape(t, d)


def _mamba_in_kernel(h_ref, g_ref, win_ref, wc_ref, bc_ref, dtb_ref, alog_ref,
                     expand_ref, z_ref, xs_ref, xdt_ref, bt_ref, c_ref, cs_ref, cst_ref,
                     xcat_ref, hist_ref, bact_ref, dte_ref, *, proj_chunk, row_chunk, col_chunk):
    i = pl.program_id(1)
    tm = h_ref.shape[0]
    di = xs_ref.shape[1]
    gn = c_ref.shape[1]
    L = SSM_CHUNK
    halo = SUBLANES
    n_xbc = di + 2 * gn
    u = _rms(h_ref[...], g_ref[...]).astype(BF16)
    dt = jax.nn.softplus(_dot(u, win_ref[:, di + n_xbc:]) + dtb_ref[...])

    @pl.when(i == 0)
    def _():
        hist_ref[...] = jnp.zeros(hist_ref.shape, F32)

    def project_z(q):
        z_ref[:, q * proj_chunk:(q + 1) * proj_chunk] = _dot(u, win_ref[:, q * proj_chunk:(q + 1) * proj_chunk])

    def project_xbc(panel, slot):
        p0 = di + panel * proj_chunk
        xcat_ref[slot, 0:halo, :] = hist_ref[panel]
        xcat_ref[slot, halo:, :] = _dot(u, win_ref[:, p0:p0 + proj_chunk])

    def time_sums():
        r_i = lax.broadcasted_iota(jnp.int32, (L, L), 0)
        c_i = lax.broadcasted_iota(jnp.int32, (L, L), 1)
        tri = (r_i >= c_i).astype(F32)
        a_row = -jnp.exp(alog_ref[...])
        for c in range(tm // L):
            cs = jnp.dot(tri, dt[c * L:(c + 1) * L, :] * a_row, precision=lax.Precision.HIGHEST,
                         preferred_element_type=F32)
            cs_ref[c * L:(c + 1) * L, :] = cs
            cst_ref[c] = cs.T

    dt_hi = dt.astype(BF16)
    rem = dt - dt_hi.astype(F32)
    dt_mid = rem.astype(BF16)
    dt_lo = (rem - dt_mid.astype(F32)).astype(BF16)
    dt_terms = jnp.concatenate([dt_hi, dt_mid, dt_lo], axis=1)

    def expand_dt(panel):
        p0 = panel * proj_chunk
        dte_ref[...] = _dot(dt_terms, expand_ref[:, p0:p0 + proj_chunk])

    def conv_silu(panel, slot):
        p0 = panel * proj_chunk
        hist_ref[panel] = xcat_ref[slot, tm:tm + halo, :]
        for r0 in range(0, tm, row_chunk):
            for c0 in range(p0, p0 + proj_chunk, col_chunk):
                rows = slice(r0, r0 + row_chunk)
                cols = slice(c0, c0 + col_chunk)
                xb = xcat_ref[slot, r0:r0 + halo + row_chunk, c0 - p0:c0 - p0 + col_chunk]
                x1 = pltpu.roll(xb, 1, axis=0)
                near = wc_ref[3:4, cols] * xb + wc_ref[2:3, cols] * x1
                far = wc_ref[1:2, cols] * xb + wc_ref[0:1, cols] * x1
                acc = near[halo:, :] + pltpu.roll(far, 2, axis=0)[halo:, :] + bc_ref[:, cols]
                act = acc * jax.nn.sigmoid(acc)
                if c0 < di:
                    xs_ref[rows, cols] = act
                    xdt_ref[rows, cols] = (act * dte_ref[rows, c0 - p0:c0 - p0 + col_chunk]).astype(BF16)
                elif c0 < di + gn:
                    bact_ref[rows, c0 - di:c0 - di + col_chunk] = act
                else:
                    c_ref[rows, c0 - di - gn:c0 - di - gn + col_chunk] = act.astype(BF16)

    def transpose_b():
        for c in range(tm // L):
            for g in range(SSM_GROUPS):
                n0 = g * SSM_STATE
                bt_ref[c, n0:n0 + SSM_STATE, :] = bact_ref[c * L:(c + 1) * L, n0:n0 + SSM_STATE].T.astype(BF16)

    n_x, n_b = di // proj_chunk, gn // proj_chunk
    order = list(range(n_x, n_xbc // proj_chunk)) + list(range(n_x))
    project_xbc(order[0], 0)
    for step, panel in enumerate(order):
        if step + 1 < len(order):
            project_xbc(order[step + 1], (step + 1) % 2)
        if step < n_x:
            project_z(step)
        if step == 0:
            time_sums()
        if panel < n_x:
            expand_dt(panel)
        conv_silu(panel, step % 2)
        if panel == n_x + n_b - 1:
            transpose_b()


def _ssd_kernel(h_ref, z_ref, xs_ref, xdt_ref, bt_ref, c_ref, cs_ref, cst_ref, dexp_ref, ng_ref,
                wout_ref, o_ref, state_ref, y_ref, yn_ref):
    i = pl.program_id(1)
    tm = h_ref.shape[0]
    L = SSM_CHUNK
    gw = 4 * SSM_HEAD_DIM

    @pl.when(i == 0)
    def _():
        state_ref[...] = jnp.zeros(state_ref.shape, F32)

    r_i = lax.broadcasted_iota(jnp.int32, (L, L), 0)
    c_i = lax.broadcasted_iota(jnp.int32, (L, L), 1)
    causal = r_i >= c_i
    left = _left_half()

    def chunk(c, carry):
        r0 = pl.multiple_of(c * L, L)
        cs = cs_ref[pl.ds(r0, L), :]
        cs_t = cst_ref[c]
        for g in range(SSM_GROUPS):
            n0 = g * SSM_STATE
            bg_t = bt_ref[c, n0:n0 + SSM_STATE, :]
            cg = c_ref[pl.ds(r0, L), n0:n0 + SSM_STATE]
            cb = _dot(cg, bg_t)
            xg = xdt_ref[pl.ds(r0, L), g * gw:(g + 1) * gw].astype(F32)
            y_tiles, ecs_tiles, dec_tiles = [], [], []
            for pair in range(2):
                xt = xg[:, pair * LANES:(pair + 1) * LANES]
                ms, bcs = [], []
                for half in range(2):
                    hh = 4 * g + 2 * pair + half
                    bc = jnp.broadcast_to(cs[:, hh:hh + 1], (L, L))
                    lm = jnp.exp(jnp.where(causal, bc - cs_t[hh:hh + 1, :], -jnp.inf))
                    ms.append((cb * lm).astype(BF16))
                    bcs.append(bc)
                x_stack = jnp.concatenate([jnp.where(left, xt, 0.0), jnp.where(left, 0.0, xt)],
                                          axis=0).astype(BF16)
                y_tiles.append(_dot(jnp.concatenate(ms, axis=1), x_stack))
                cs_e = jnp.where(left, bcs[0], bcs[1])
                ecs_tiles.append(jnp.exp(cs_e))
                dec_tiles.append(jnp.exp(cs_e[L - 1:L, :] - cs_e))
            ecs_e = jnp.concatenate(ecs_tiles, axis=1)
            st = state_ref[g]
            y_off = _dot(cg, st.astype(BF16)) * ecs_e
            xw = (xg * jnp.concatenate(dec_tiles, axis=1)).astype(BF16)
            state_ref[g] = st * ecs_e[L - 1:L, :] + _dot(bg_t, xw)
            cols = slice(g * gw, (g + 1) * gw)
            zz = z_ref[pl.ds(r0, L), cols]
            yg = jnp.concatenate(y_tiles, axis=1) + y_off + xs_ref[pl.ds(r0, L), cols] * dexp_ref[:, cols]
            yg = yg * (zz * jax.nn.sigmoid(zz))
            y_ref[pl.ds(r0, L), cols] = yg
            sq = yg * yg
            sq = sq[:, :LANES] + sq[:, LANES:]
            ssq = sq if g == 0 else ssq + sq
        inv = lax.rsqrt(jnp.sum(ssq, axis=-1, keepdims=True) * (1.0 / y_ref.shape[1]) + EPS)
        yn_ref[pl.ds(r0, L), :] = (y_ref[pl.ds(r0, L), :] * inv * ng_ref[...]).astype(BF16)
        return carry

    lax.fori_loop(0, tm // L, chunk, 0)
    o_ref[...] = h_ref[...] + _dot(yn_ref[...], wout_ref[...])


def _mamba_layer(h, batch, seq, g, win, wc, bc, dtb, alog, dexp, ng, wout, *,
                 tm_in=512, tm_ssd=512, proj_chunk=512):
    t, d = h.shape
    di = wout.shape[0]
    gn = SSM_GROUPS * SSM_STATE
    L = SSM_CHUNK
    tm = min(tm_in, seq)
    nt = seq // tm
    cur = lambda b, i: (b * nt + i, 0)
    cur3 = lambda b, i: (b * nt + i, 0, 0)
    expand = jnp.tile(jnp.repeat(jnp.eye(LANES, di // SSM_HEAD_DIM, dtype=BF16), SSM_HEAD_DIM, axis=1),
                      (3, 1))
    z, xs, xdt, bt, cm, cs, cst = pl.pallas_call(
        functools.partial(_mamba_in_kernel, proj_chunk=proj_chunk, row_chunk=64, col_chunk=256),
        grid=(batch, nt),
        in_specs=[pl.BlockSpec((tm, d), cur), _const_spec(g.shape), _const_spec(win.shape),
                  _const_spec(wc.shape), _const_spec(bc.shape), _const_spec(dtb.shape),
                  _const_spec(alog.shape), _const_spec(expand.shape)],
        out_specs=[pl.BlockSpec((tm, di), cur), pl.BlockSpec((tm, di), cur),
                   pl.BlockSpec((tm, di), cur), pl.BlockSpec((tm // L, gn, L), cur3),
                   pl.BlockSpec((tm, gn), cur), pl.BlockSpec((tm, LANES), cur),
                   pl.BlockSpec((tm // L, LANES, L), cur3)],
        out_shape=[jax.ShapeDtypeStruct((t, di), F32), jax.ShapeDtypeStruct((t, di), F32),
                   jax.ShapeDtypeStruct((t, di), BF16), jax.ShapeDtypeStruct((t // L, gn, L), BF16),
                   jax.ShapeDtypeStruct((t, gn), BF16), jax.ShapeDtypeStruct((t, LANES), F32),
                   jax.ShapeDtypeStruct((t // L, LANES, L), F32)],
        scratch_shapes=[pltpu.VMEM((2, tm + SUBLANES, proj_chunk), F32),
                        pltpu.VMEM(((di + 2 * gn) // proj_chunk, SUBLANES, proj_chunk), F32),
                        pltpu.VMEM((tm, gn), F32),
                        pltpu.VMEM((tm, proj_chunk), F32)],
        compiler_params=_params(2),
        name="mamba_in",
    )(h, g, win, wc, bc, dtb, alog, expand)

    tm = min(tm_ssd, seq)
    nt = seq // tm
    cur = lambda b, i: (b * nt + i, 0)
    cur3 = lambda b, i: (b * nt + i, 0, 0)
    return pl.pallas_call(
        _ssd_kernel,
        grid=(batch, nt),
        in_specs=[pl.BlockSpec((tm, d), cur), pl.BlockSpec((tm, di), cur),
                  pl.BlockSpec((tm, di), cur), pl.BlockSpec((tm, di), cur),
                  pl.BlockSpec((tm // L, gn, L), cur3), pl.BlockSpec((tm, gn), cur),
                  pl.BlockSpec((tm, LANES), cur), pl.BlockSpec((tm // L, LANES, L), cur3),
                  _const_spec(dexp.shape), _const_spec(ng.shape), _const_spec(wout.shape)],
        out_specs=pl.BlockSpec((tm, d), cur),
        out_shape=jax.ShapeDtypeStruct((t, d), F32),
        scratch_shapes=[pltpu.VMEM((SSM_GROUPS, SSM_STATE, 4 * SSM_HEAD_DIM), F32),
                        pltpu.VMEM((tm, di), F32), pltpu.VMEM((tm, di), BF16)],
        compiler_params=_params(2),
        name="mamba_ssd",
    )(h, z, xs, xdt, bt, cm, cs, cst, dexp, ng, wout)


def _row(v):
    return v.reshape(1, -1).astype(F32)


def _pad_cols(a, n):
    return jnp.pad(a, ((0, 0), (0, n - a.shape[1])))


def kernel(x, p, mix_norm_g, mlp_norm_g, ple_norm_g, a_wqkv, a_q_norm_g, a_k_norm_g, a_sinks, a_wo, b_w_pw1, b_b_pw1, b_w_dw, b_b_dw, b_ln_g, b_ln_b, b_w_pw2, b_b_pw2, c_w_in, c_w_conv, c_b_conv, c_dt_bias, c_A_log, c_D, c_norm_g, c_w_out, m_w1, m_w2, ple_w_proj, ple_w_gate):
    batch, seq, d = x.shape
    depth = p.shape[0]
    t = batch * seq
    h = x.reshape(t, d)
    p = p.reshape(depth, t, -1)
    w1_all, w2_all = m_w1.astype(BF16), m_w2.astype(BF16)
    wg_all, wp_all = ple_w_gate.astype(BF16), ple_w_proj.astype(BF16)
    for i in range(depth):
        kind, j = i % N_MIXERS, i // N_MIXERS
        g = _row(mix_norm_g[i])
        if kind == 0:
            scale = ATTN_HEAD_DIM ** -0.5
            gq = _row(jnp.tile(a_q_norm_g[j] * scale, 2))
            gk = _row(jnp.tile(a_k_norm_g[j], 2))
            h = _attention_layer(h, batch, seq, g, a_wqkv[j].astype(BF16), gq, gk,
                                 a_sinks[j].astype(F32), a_wo[j].astype(BF16))
        elif kind == 1:
            wdw = jnp.pad(b_w_dw[j], ((0, CONV_HALO - CONV_WIDTH), (0, 0)))
            h = _conformer_layer(h, batch, seq, g, b_w_pw1[j].astype(BF16), _row(b_b_pw1[j]),
                                 wdw, _row(b_b_dw[j]), _row(b_ln_g[j]), _row(b_ln_b[j]),
                                 b_w_pw2[j].astype(BF16), _row(b_b_pw2[j]))
        else:
            di = c_w_out.shape[1]
            gn = SSM_GROUPS * SSM_STATE
            win = _pad_cols(c_w_in[j], 2 * di + 2 * gn + LANES).astype(BF16)
            wc = jnp.pad(c_w_conv[j], ((0, SUBLANES - SSM_CONV), (0, 0)))
            dtb = _pad_cols(_row(c_dt_bias[j]), LANES)
            alog = _pad_cols(_row(c_A_log[j]), LANES)
            dexp = _row(jnp.repeat(c_D[j], SSM_HEAD_DIM))
            h = _mamba_layer(h, batch, seq, g, win, wc, _row(c_b_conv[j]), dtb, alog,
                             dexp, _row(c_norm_g[j]), c_w_out[j].astype(BF16))
        h = _mlp_ple(h, p, i, _row(mlp_norm_g[i]), _row(ple_norm_g[i]), w1_all, w2_all, wg_all, wp_all)
    return h.reshape(batch, seq, d)
```

```python
import functools

import jax
import jax.numpy as jnp
from jax import lax
from jax.experimental import pallas as pl
from jax.experimental.pallas import tpu as pltpu

F32 = jnp.float32
BF16 = jnp.bfloat16
EPS = 1e-6
LOG2_E = 1.4426950408889634

N_MIXERS = 3
ATTN_HEADS = 16
ATTN_KV_HEADS = 4
ATTN_HEAD_DIM = 64
ATTN_BLOCK = 128
CONV_WIDTH = 31
SSM_HEAD_DIM = 64
SSM_GROUPS = 8
SSM_STATE = 128
SSM_CONV = 4
SSM_CHUNK = 128

LANES = 128
SUBLANES = 8
VMEM_LIMIT_BYTES = 56 * 1024 * 1024


def _params(n_grid_dims):
    return pltpu.CompilerParams(
        dimension_semantics=("arbitrary",) * n_grid_dims,
        vmem_limit_bytes=VMEM_LIMIT_BYTES)


def _const_spec(shape):
    nd = len(shape)
    return pl.BlockSpec(shape, lambda *_: (0,) * nd, pipeline_mode=pl.Buffered(1))


def _rms(x, g):
    return x * lax.rsqrt(jnp.mean(x * x, axis=-1, keepdims=True) + EPS) * g


def _dot(a, b):
    return jnp.dot(a, b, preferred_element_type=F32)


def _dot_nt(a, b):
    return lax.dot_general(a, b, (((1,), (1,)), ((), ())), preferred_element_type=F32)


def _left_half():
    return lax.broadcasted_iota(jnp.int32, (1, LANES), 1) < (LANES // 2)


def _mlp_ple_kernel(h_ref, p_ref, g1_ref, g2_ref, w1_ref, w2_ref, wg_ref, wp_ref, o_ref, *,
                    hidden_chunk):
    x = h_ref[...]
    u = _rms(x, g1_ref[...]).astype(BF16)
    acc = jnp.zeros_like(x)
    for c in range(0, w1_ref.shape[1], hidden_chunk):
        a = _dot(u, w1_ref[:, c:c + hidden_chunk])
        a = jnp.square(jnp.maximum(a, 0.0)).astype(BF16)
        acc = acc + _dot(a, w2_ref[c:c + hidden_chunk, :])
    h2 = x + acc
    u2 = _rms(h2, g2_ref[...]).astype(BF16)
    gate = jax.nn.sigmoid(_dot(u2, wg_ref[...]))
    proj = _dot(p_ref[...].astype(BF16), wp_ref[...])
    o_ref[...] = h2 + gate * proj


def _layer_spec(stack, layer):
    return pl.BlockSpec((None,) + stack.shape[1:], lambda *_: (layer, 0, 0), pipeline_mode=pl.Buffered(1))


def _mlp_ple(h, p, layer, g1, g2, w1, w2, wg, wp, *, tm=1024, hidden_chunk=512):
    t, d = h.shape
    tm = min(tm, t)
    row = lambda i: (i, 0)
    return pl.pallas_call(
        functools.partial(_mlp_ple_kernel, hidden_chunk=hidden_chunk),
        grid=(t // tm,),
        in_specs=[pl.BlockSpec((tm, d), row),
                  pl.BlockSpec((None, tm, p.shape[2]), lambda i: (layer, i, 0)),
                  _const_spec(g1.shape), _const_spec(g2.shape), _layer_spec(w1, layer),
                  _layer_spec(w2, layer), _layer_spec(wg, layer), _layer_spec(wp, layer)],
        out_specs=pl.BlockSpec((tm, d), row),
        out_shape=jax.ShapeDtypeStruct((t, d), F32),
        compiler_params=_params(1),
        name="mlp_ple",
    )(h, p, g1, g2, w1, w2, wg, wp)


def _qkv_kernel(h_ref, g_ref, w_ref, gq_ref, gk_ref, seg_ref, q_ref, ke_ref, ko_ref, v2_ref, *, panel):
    hd = ATTN_HEAD_DIM
    nq = q_ref.shape[1]
    nk = ATTN_KV_HEADS * hd
    u = _rms(h_ref[...], g_ref[...]).astype(BF16)
    left = _left_half()

    def pair_norm(t, g):
        sq = t * t
        sq_hi = sq.astype(BF16)
        sq_lo = (sq - sq_hi.astype(F32)).astype(BF16)
        ssq = _dot(jnp.concatenate([sq_hi, sq_lo], axis=1), seg_ref[...])
        return t * lax.rsqrt(ssq * (1.0 / hd) + EPS) * g

    def project(c0):
        return _dot(u, w_ref[:, c0:c0 + panel])

    def finish(c0, acc):
        for t0 in range(0, panel, LANES):
            c = c0 + t0
            t = acc[:, t0:t0 + LANES]
            if c < nq:
                q_ref[:, c:c + LANES] = pair_norm(t, gq_ref[...]).astype(BF16)
                continue
            j = (c - nq) % nk // LANES
            first, second = 2 * j * LANES, (2 * j + 1) * LANES
            if c < nq + nk:
                t = pair_norm(t, gk_ref[...])
                r = pltpu.roll(t, hd, axis=1)
                zero = jnp.zeros_like(t)
                ke_ref[:, first:first + LANES] = jnp.where(left, t, zero).astype(BF16)
                ke_ref[:, second:second + LANES] = jnp.where(left, r, zero).astype(BF16)
                ko_ref[:, first:first + LANES] = jnp.where(left, zero, r).astype(BF16)
                ko_ref[:, second:second + LANES] = jnp.where(left, zero, t).astype(BF16)
            else:
                r = pltpu.roll(t, hd, axis=1)
                v2_ref[:, first:first + LANES] = jnp.where(left, t, r).astype(BF16)
                v2_ref[:, second:second + LANES] = jnp.where(left, r, t).astype(BF16)

    starts = list(range(0, w_ref.shape[1], panel))
    acc = project(starts[0])
    for idx, c0 in enumerate(starts):
        nxt = project(starts[idx + 1]) if idx + 1 < len(starts) else None
        finish(c0, acc)
        acc = nxt


def _attn_core(sink_ref, bias_ref, h_ref, q_ref, ke_ref, ko_ref, v_ref, wo_ref, o_ref, oacc_ref):
    blk = ATTN_BLOCK
    group = ATTN_HEADS // ATTN_KV_HEADS
    i = pl.program_id(1)
    tq = q_ref.shape[0]

    row = lax.broadcasted_iota(jnp.int32, (blk, blk), 0)
    col = lax.broadcasted_iota(jnp.int32, (blk, blk), 1)
    lower = col <= row
    no_prev = (col - row) > jnp.where(i > 0, blk, 0)
    left = _left_half()
    for b0 in range(0, tq, blk):
        for kv in range(ATTN_KV_HEADS):
            lanes = slice(kv * LANES, (kv + 1) * LANES)
            qt = jnp.concatenate(
                [q_ref[b0:b0 + blk, (2 * kv) * LANES:(2 * kv + 1) * LANES],
                 q_ref[b0:b0 + blk, (2 * kv + 1) * LANES:(2 * kv + 2) * LANES]], axis=0)
            s_even = _dot_nt(qt, ke_ref[b0:b0 + 2 * blk, lanes])
            s_odd = _dot_nt(qt, ko_ref[b0:b0 + 2 * blk, lanes])
            p_cur, p_prev, r_den = [], [], []
            for g in range(group):
                src = s_even if g % 2 == 0 else s_odd
                r0 = (g // 2) * blk
                hh = kv * group + g
                s = jnp.where(lower, src[r0:r0 + blk, blk:], src[r0:r0 + blk, :blk]) + bias_ref[hh]
                if b0 == 0:
                    s = jnp.where(no_prev, -jnp.inf, s)
                sink = sink_ref[hh]
                m = jnp.maximum(jnp.max(s, axis=-1, keepdims=True), sink)
                e = jnp.exp(s - m)
                r_den.append(1.0 / (jnp.sum(e, axis=-1, keepdims=True) + jnp.exp(sink - m)))
                p_cur.append(jnp.where(lower, e, 0.0).astype(BF16))
                p_prev.append(jnp.where(lower, 0.0, e).astype(BF16))
            o4 = (_dot(jnp.concatenate(p_cur, axis=0), v_ref[b0 + blk:b0 + 2 * blk, lanes])
                  + _dot(jnp.concatenate(p_prev, axis=0), v_ref[b0:b0 + blk, lanes]))
            for pair in range(group // 2):
                ga, gb = 2 * pair, 2 * pair + 1
                oa = o4[ga * blk:(ga + 1) * blk, :] * r_den[ga]
                ob = o4[gb * blk:(gb + 1) * blk, :] * r_den[gb]
                c0 = (2 * kv + pair) * LANES
                oacc_ref[b0:b0 + blk, c0:c0 + LANES] = jnp.where(left, oa, ob).astype(BF16)
    o_ref[...] = h_ref[...] + _dot(oacc_ref[...], wo_ref[...])


def _alibi_bias():
    blk = ATTN_BLOCK
    row = jnp.arange(blk, dtype=jnp.int32)[:, None]
    col = jnp.arange(blk, dtype=jnp.int32)[None, :]
    rel = jnp.where(col <= row, row - col, row - col + blk).astype(F32)
    slopes = jnp.exp2(-8.0 * (jnp.arange(ATTN_HEADS, dtype=F32) + 1.0) / ATTN_HEADS)
    return -(slopes[:, None, None] * rel[None])


def _attention_layer(h, batch, seq, g, wqkv, gq, gk, sinks, wo, *, tm=512):
    t, d = h.shape
    hd, blk = ATTN_HEAD_DIM, ATTN_BLOCK
    nq, nkp = ATTN_HEADS * hd, ATTN_KV_HEADS * LANES
    tm = min(tm, seq)
    lane_head = jnp.arange(LANES) // hd
    seg = jnp.tile((lane_head[:, None] == lane_head[None, :]).astype(BF16), (2, 1))
    bias = _alibi_bias()
    nt = seq // tm
    cur = lambda b, i: (b * nt + i, 0)

    def fused(sink_ref, bias_ref, h_ref, g_ref, wqkv_ref, gq_ref, gk_ref, seg_ref, wo_ref, o_ref,
              q_ref, ke_ref, ko_ref, v_ref, oacc_ref):
        i = pl.program_id(1)
        kv_bufs = (ke_ref, ko_ref, v_ref)

        @pl.when(i == 0)
        def _():
            for buf in kv_bufs:
                buf[0:blk, :] = jnp.zeros((blk, nkp), BF16)

        _qkv_kernel(h_ref, g_ref, wqkv_ref, gq_ref, gk_ref, seg_ref, q_ref,
                    *(buf.at[blk:, :] for buf in kv_bufs), panel=2 * LANES)
        _attn_core(sink_ref, bias_ref, h_ref, q_ref, ke_ref, ko_ref, v_ref, wo_ref, o_ref, oacc_ref)
        for buf in kv_bufs:
            buf[0:blk, :] = buf[tm:tm + blk, :]

    return pl.pallas_call(
        fused,
        grid=(batch, nt),
        in_specs=[pl.BlockSpec(memory_space=pltpu.SMEM), _const_spec(bias.shape),
                  pl.BlockSpec((tm, d), cur), _const_spec(g.shape), _const_spec(wqkv.shape),
                  _const_spec(gq.shape), _const_spec(gk.shape), _const_spec(seg.shape),
                  _const_spec(wo.shape)],
        out_specs=pl.BlockSpec((tm, d), cur),
        out_shape=jax.ShapeDtypeStruct((t, d), F32),
        scratch_shapes=[pltpu.VMEM((tm, nq), BF16), pltpu.VMEM((tm + blk, nkp), BF16),
                        pltpu.VMEM((tm + blk, nkp), BF16), pltpu.VMEM((tm + blk, nkp), BF16),
                        pltpu.VMEM((tm, nq), BF16)],
        compiler_params=_params(2),
        name="attention",
    )(sinks, bias, h, g, wqkv, gq, gk, seg, wo)


CONV_HALO = 32


def _conformer_kernel(h_ref, g_ref, w1_ref, b1_ref, wdw_ref, bdw_ref, lg_ref, lb_ref, w2_ref, b2_ref,
                      o_ref, ycat_ref, ysh_ref, conv_ref, *, row_chunk, col_chunk):
    i = pl.program_id(1)
    n_streams, tm, d = h_ref.shape
    halo = CONV_HALO
    span = tm + halo - SUBLANES
    first = halo - (CONV_WIDTH - 1)

    @pl.when(i == 0)
    def _():
        ycat_ref[:, 0:halo, :] = jnp.zeros((n_streams, halo, d), F32)

    for s in range(n_streams):
        u = _rms(h_ref[s], g_ref[...]).astype(BF16)
        a = _dot(u, w1_ref[...]) + b1_ref[...]
        ycat_ref[s, halo:, :] = a[:, :d] * jax.nn.sigmoid(a[:, d:])

    for s in range(n_streams):
        for ph in range(1, SUBLANES):
            ysh_ref[s, ph - 1] = ycat_ref[s, ph:ph + span, :]
        for r0 in range(0, tm, row_chunk):
            for c0 in range(0, d, col_chunk):
                cols = slice(c0, c0 + col_chunk)
                acc = jnp.zeros((row_chunk, col_chunk), F32)
                for k in range(CONV_WIDTH):
                    phase = (first + k) % SUBLANES
                    base = r0 + first + k - phase
                    if phase == 0:
                        tap = ycat_ref[s, base:base + row_chunk, cols]
                    else:
                        tap = ysh_ref[s, phase - 1, base:base + row_chunk, cols]
                    acc = acc + wdw_ref[k:k + 1, cols] * tap
                conv_ref[s, r0:r0 + row_chunk, cols] = acc + bdw_ref[:, cols]
        ycat_ref[s, 0:halo, :] = ycat_ref[s, tm:tm + halo, :]

    for s in range(n_streams):
        c = conv_ref[s]
        mu = jnp.mean(c, axis=-1, keepdims=True)
        cc = c - mu
        var = jnp.mean(cc * cc, axis=-1, keepdims=True)
        y = cc * lax.rsqrt(var + EPS) * lg_ref[...] + lb_ref[...]
        y = y * jax.nn.sigmoid(y)
        o_ref[s] = h_ref[s] + _dot(y.astype(BF16), w2_ref[...]) + b2_ref[...]


def _conformer_layer(h, batch, seq, g, w1, b1, wdw, bdw, lg, lb, w2, b2, *, tm=256, n_streams=2):
    t, d = h.shape
    tm = min(tm, seq)
    n_streams = min(n_streams, batch)
    cur = lambda b, i: (b, i, 0)
    out = pl.pallas_call(
        functools.partial(_conformer_kernel, row_chunk=64, col_chunk=256),
        grid=(batch // n_streams, seq // tm),
        in_specs=[pl.BlockSpec((n_streams, tm, d), cur), _const_spec(g.shape), _const_spec(w1.shape),
                  _const_spec(b1.shape), _const_spec(wdw.shape), _const_spec(bdw.shape),
                  _const_spec(lg.shape), _const_spec(lb.shape), _const_spec(w2.shape),
                  _const_spec(b2.shape)],
        out_specs=pl.BlockSpec((n_streams, tm, d), cur),
        out_shape=jax.ShapeDtypeStruct((batch, seq, d), F32),
        scratch_shapes=[pltpu.VMEM((n_streams, tm + CONV_HALO, d), F32),
                        pltpu.VMEM((n_streams, SUBLANES - 1, tm + CONV_HALO - SUBLANES, d), F32),
                        pltpu.VMEM((n_streams, tm, d), F32)],
        compiler_params=_params(2),
        name="conformer",
    )(h.reshape(batch, seq, d), g, w1, b1, wdw, bdw, lg, lb, w2, b2)
    return out.reshape(t, d)


def _mamba_in_kernel(h_ref, g_ref, win_ref, wc_ref, bc_ref, dtb_ref, alog_ref,
                     expand_ref, z_ref, xs_ref, xdt_ref, bt_ref, c_ref, cs_ref, cst_ref,
                     xcat_ref, hist_ref, bact_ref, dte_ref, *, proj_chunk, row_chunk, col_chunk):
    i = pl.program_id(1)
    tm = h_ref.shape[0]
    di = xs_ref.shape[1]
    gn = c_ref.shape[1]
    L = SSM_CHUNK
    halo = SUBLANES
    n_xbc = di + 2 * gn
    u = _rms(h_ref[...], g_ref[...]).astype(BF16)
    dt = jax.nn.softplus(_dot(u, win_ref[:, di + n_xbc:]) + dtb_ref[...])

    @pl.when(i == 0)
    def _():
        hist_ref[...] = jnp.zeros(hist_ref.shape, F32)

    def project_z(q):
        z_ref[:, q * proj_chunk:(q + 1) * proj_chunk] = _dot(u, win_ref[:, q * proj_chunk:(q + 1) * proj_chunk])

    def project_xbc(panel, slot):
        p0 = di + panel * proj_chunk
        xcat_ref[slot, 0:halo, :] = hist_ref[panel]
        xcat_ref[slot, halo:, :] = _dot(u, win_ref[:, p0:p0 + proj_chunk])

    def time_sums():
        r_i = lax.broadcasted_iota(jnp.int32, (L, L), 0)
        c_i = lax.broadcasted_iota(jnp.int32, (L, L), 1)
        tri = (r_i >= c_i).astype(F32)
        a_row = -jnp.exp(alog_ref[...]) * LOG2_E
        for c in range(tm // L):
            cs = jnp.dot(tri, dt[c * L:(c + 1) * L, :] * a_row, precision=lax.Precision.HIGHEST,
                         preferred_element_type=F32)
            cs_ref[c * L:(c + 1) * L, :] = cs
            cst_ref[c] = cs.T

    dt_hi = dt.astype(BF16)
    rem = dt - dt_hi.astype(F32)
    dt_mid = rem.astype(BF16)
    dt_lo = (rem - dt_mid.astype(F32)).astype(BF16)
    dt_terms = jnp.concatenate([dt_hi, dt_mid, dt_lo], axis=1)

    def expand_dt(panel):
        p0 = panel * proj_chunk
        dte_ref[...] = _dot(dt_terms, expand_ref[:, p0:p0 + proj_chunk])

    def conv_silu(panel, slot):
        p0 = panel * proj_chunk
        hist_ref[panel] = xcat_ref[slot, tm:tm + halo, :]
        for r0 in range(0, tm, row_chunk):
            for c0 in range(p0, p0 + proj_chunk, col_chunk):
                rows = slice(r0, r0 + row_chunk)
                cols = slice(c0, c0 + col_chunk)
                xb = xcat_ref[slot, r0:r0 + halo + row_chunk, c0 - p0:c0 - p0 + col_chunk]
                x1 = pltpu.roll(xb, 1, axis=0)
                near = wc_ref[3:4, cols] * xb + wc_ref[2:3, cols] * x1
                far = wc_ref[1:2, cols] * xb + wc_ref[0:1, cols] * x1
                acc = near[halo:, :] + pltpu.roll(far, 2, axis=0)[halo:, :] + bc_ref[:, cols]
                act = acc * jax.nn.sigmoid(acc)
                if c0 < di:
                    xs_ref[rows, cols] = act
                    xdt_ref[rows, cols] = (act * dte_ref[rows, c0 - p0:c0 - p0 + col_chunk]).astype(BF16)
                elif c0 < di + gn:
                    bact_ref[rows, c0 - di:c0 - di + col_chunk] = act
                else:
                    c_ref[rows, c0 - di - gn:c0 - di - gn + col_chunk] = act.astype(BF16)

    def transpose_b():
        for c in range(tm // L):
            for g in range(SSM_GROUPS):
                n0 = g * SSM_STATE
                bt_ref[c, n0:n0 + SSM_STATE, :] = bact_ref[c * L:(c + 1) * L, n0:n0 + SSM_STATE].T.astype(BF16)

    n_x, n_b = di // proj_chunk, gn // proj_chunk
    order = list(range(n_x, n_xbc // proj_chunk)) + list(range(n_x))
    project_xbc(order[0], 0)
    for step, panel in enumerate(order):
        if step + 1 < len(order):
            project_xbc(order[step + 1], (step + 1) % 2)
        if step < n_x:
            project_z(step)
        if step == 0:
            time_sums()
        if panel < n_x:
            expand_dt(panel)
        conv_silu(panel, step % 2)
        if panel == n_x + n_b - 1:
            transpose_b()


def _ssd_kernel(h_ref, z_ref, xs_ref, xdt_ref, bt_ref, c_ref, cs_ref, cst_ref, dexp_ref, ng_ref,
                wout_ref, o_ref, state_ref, y_ref, yn_ref):
    i = pl.program_id(1)
    tm = h_ref.shape[0]
    L = SSM_CHUNK
    gw = 4 * SSM_HEAD_DIM

    @pl.when(i == 0)
    def _():
        state_ref[...] = jnp.zeros(state_ref.shape, F32)

    r_i = lax.broadcasted_iota(jnp.int32, (L, L), 0)
    c_i = lax.broadcasted_iota(jnp.int32, (L, L), 1)
    causal = r_i >= c_i
    left = _left_half()

    def chunk(c, carry):
        r0 = pl.multiple_of(c * L, L)
        cs = cs_ref[pl.ds(r0, L), :]
        cs_t = cst_ref[c]
        for g in range(SSM_GROUPS):
            n0 = g * SSM_STATE
            bg_t = bt_ref[c, n0:n0 + SSM_STATE, :]
            cg = c_ref[pl.ds(r0, L), n0:n0 + SSM_STATE]
            cb = _dot(cg, bg_t)
            xg = xdt_ref[pl.ds(r0, L), g * gw:(g + 1) * gw]
            y_tiles, ecs_tiles, dec_tiles = [], [], []
            for pair in range(2):
                xt = xg[:, pair * LANES:(pair + 1) * LANES]
                ms, bcs = [], []
                for half in range(2):
                    hh = 4 * g + 2 * pair + half
                    bc = jnp.broadcast_to(cs[:, hh:hh + 1], (L, L))
                    lm = jnp.exp2(jnp.where(causal, bc - cs_t[hh:hh + 1, :], -jnp.inf))
                    ms.append((cb * lm).astype(BF16))
                    bcs.append(bc)
                prod = _dot(jnp.concatenate(ms, axis=0), xt)
                y_tiles.append(jnp.where(left, prod[:L, :], prod[L:, :]))
                cs_e = jnp.where(left, bcs[0], bcs[1])
                ecs_tiles.append(jnp.exp2(cs_e))
                dec_tiles.append(jnp.exp2(cs_e[L - 1:L, :] - cs_e))
            ecs_e = jnp.concatenate(ecs_tiles, axis=1)
            st = state_ref[g]
            y_off = _dot(cg, st.astype(BF16)) * ecs_e
            xw = (xg.astype(F32) * jnp.concatenate(dec_tiles, axis=1)).astype(BF16)
            state_ref[g] = st * ecs_e[L - 1:L, :] + _dot(bg_t, xw)
            cols = slice(g * gw, (g + 1) * gw)
            zz = z_ref[pl.ds(r0, L), cols]
            yg = jnp.concatenate(y_tiles, axis=1) + y_off + xs_ref[pl.ds(r0, L), cols] * dexp_ref[:, cols]
            yg = yg * (zz * jax.nn.sigmoid(zz))
            y_ref[pl.ds(r0, L), cols] = yg
            sq = yg * yg
            sq = sq[:, :LANES] + sq[:, LANES:]
            ssq = sq if g == 0 else ssq + sq
        inv = lax.rsqrt(jnp.sum(ssq, axis=-1, keepdims=True) * (1.0 / y_ref.shape[1]) + EPS)
        yn_ref[pl.ds(r0, L), :] = (y_ref[pl.ds(r0, L), :] * inv * ng_ref[...]).astype(BF16)
        return carry

    lax.fori_loop(0, tm // L, chunk, 0)
    o_ref[...] = h_ref[...] + _dot(yn_ref[...], wout_ref[...])


def _mamba_layer(h, batch, seq, g, win, wc, bc, dtb, alog, dexp, ng, wout, *,
                 tm=512, proj_chunk=512):
    t, d = h.shape
    di = wout.shape[0]
    gn = SSM_GROUPS * SSM_STATE
    L = SSM_CHUNK
    tm = min(tm, seq)
    nt = seq // tm
    cur = lambda b, i: (b * nt + i, 0)
    expand = jnp.tile(jnp.repeat(jnp.eye(LANES, di // SSM_HEAD_DIM, dtype=BF16), SSM_HEAD_DIM, axis=1),
                      (3, 1))
    def fused(h_ref, g_ref, win_ref, wc_ref, bc_ref, dtb_ref, alog_ref, expand_ref, dexp_ref, ng_ref,
              wout_ref, o_ref, z_ref, xs_ref, xdt_ref, bt_ref, c_ref, cs_ref, cst_ref,
              xcat_ref, hist_ref, dte_ref, state_ref, y_ref, yn_ref):
        bact_ref = y_ref.at[:, 0:gn]
        _mamba_in_kernel(h_ref, g_ref, win_ref, wc_ref, bc_ref, dtb_ref, alog_ref, expand_ref,
                         z_ref, xs_ref, xdt_ref, bt_ref, c_ref, cs_ref, cst_ref,
                         xcat_ref, hist_ref, bact_ref, dte_ref,
                         proj_chunk=proj_chunk, row_chunk=64, col_chunk=256)
        _ssd_kernel(h_ref, z_ref, xs_ref, xdt_ref, bt_ref, c_ref, cs_ref, cst_ref, dexp_ref, ng_ref,
                    wout_ref, o_ref, state_ref, y_ref, yn_ref)

    return pl.pallas_call(
        fused,
        grid=(batch, nt),
        in_specs=[pl.BlockSpec((tm, d), cur), _const_spec(g.shape), _const_spec(win.shape),
                  _const_spec(wc.shape), _const_spec(bc.shape), _const_spec(dtb.shape),
                  _const_spec(alog.shape), _const_spec(expand.shape), _const_spec(dexp.shape),
                  _const_spec(ng.shape), _const_spec(wout.shape)],
        out_specs=pl.BlockSpec((tm, d), cur),
        out_shape=jax.ShapeDtypeStruct((t, d), F32),
        scratch_shapes=[
            pltpu.VMEM((tm, di), F32), pltpu.VMEM((tm, di), F32), pltpu.VMEM((tm, di), BF16),
            pltpu.VMEM((tm // L, gn, L), BF16), pltpu.VMEM((tm, gn), BF16),
            pltpu.VMEM((tm, LANES), F32), pltpu.VMEM((tm // L, LANES, L), F32),
            pltpu.VMEM((2, tm + SUBLANES, proj_chunk), F32),
            pltpu.VMEM(((di + 2 * gn) // proj_chunk, SUBLANES, proj_chunk), F32),
            pltpu.VMEM((tm, proj_chunk), F32),
            pltpu.VMEM((SSM_GROUPS, SSM_STATE, 4 * SSM_HEAD_DIM), F32),
            pltpu.VMEM((tm, di), F32), pltpu.VMEM((tm, di), BF16)],
        compiler_params=_params(2),
        name="mamba",
    )(h, g, win, wc, bc, dtb, alog, expand, dexp, ng, wout)


def _row(v):
    return v.reshape(1, -1).astype(F32)


def _pad_cols(a, n):
    return jnp.pad(a, ((0, 0), (0, n - a.shape[1])))


def kernel(x, p, mix_norm_g, mlp_norm_g, ple_norm_g, a_wqkv, a_q_norm_g, a_k_norm_g, a_sinks, a_wo, b_w_pw1, b_b_pw1, b_w_dw, b_b_dw, b_ln_g, b_ln_b, b_w_pw2, b_b_pw2, c_w_in, c_w_conv, c_b_conv, c_dt_bias, c_A_log, c_D, c_norm_g, c_w_out, m_w1, m_w2, ple_w_proj, ple_w_gate):
    batch, seq, d = x.shape
    depth = p.shape[0]
    t = batch * seq
    h = x.reshape(t, d)
    p = p.reshape(depth, t, -1)
    w1_all, w2_all = m_w1.astype(BF16), m_w2.astype(BF16)
    wg_all, wp_all = ple_w_gate.astype(BF16), ple_w_proj.astype(BF16)
    for i in range(depth):
        kind, j = i % N_MIXERS, i // N_MIXERS
        g = _row(mix_norm_g[i])
        if kind == 0:
            scale = ATTN_HEAD_DIM ** -0.5
            gq = _row(jnp.tile(a_q_norm_g[j] * scale, 2))
            gk = _row(jnp.tile(a_k_norm_g[j], 2))
            h = _attention_layer(h, batch, seq, g, a_wqkv[j].astype(BF16), gq, gk,
                                 a_sinks[j].astype(F32), a_wo[j].astype(BF16))
        elif kind == 1:
            wdw = jnp.pad(b_w_dw[j], ((0, CONV_HALO - CONV_WIDTH), (0, 0)))
            h = _conformer_layer(h, batch, seq, g, b_w_pw1[j].astype(BF16), _row(b_b_pw1[j]),
                                 wdw, _row(b_b_dw[j]), _row(b_ln_g[j]), _row(b_ln_b[j]),
                                 b_w_pw2[j].astype(BF16), _row(b_b_pw2[j]))
        else:
            di = c_w_out.shape[1]
            gn = SSM_GROUPS * SSM_STATE
            win = _pad_cols(c_w_in[j], 2 * di + 2 * gn + LANES).astype(BF16)
            wc = jnp.pad(c_w_conv[j], ((0, SUBLANES - SSM_CONV), (0, 0)))
            dtb = _pad_cols(_row(c_dt_bias[j]), LANES)
            alog = _pad_cols(_row(c_A_log[j]), LANES)
            dexp = _row(jnp.repeat(c_D[j], SSM_HEAD_DIM))
            h = _mamba_layer(h, batch, seq, g, win, wc, _row(c_b_conv[j]), dtb, alog,
                             dexp, _row(c_norm_g[j]), c_w_out[j].astype(BF16))
        h = _mlp_ple(h, p, i, _row(mlp_norm_g[i]), _row(ple_norm_g[i]), w1_all, w2_all, wg_all, wp_all)
    return h.reshape(batch, seq, d)
```

```python
import functools

import jax
import jax.numpy as jnp
from jax import lax
from jax.experimental import pallas as pl
from jax.experimental.pallas import tpu as pltpu

F32 = jnp.float32
BF16 = jnp.bfloat16
EPS = 1e-6
LOG2_E = 1.4426950408889634

N_MIXERS = 3
ATTN_HEADS = 16
ATTN_KV_HEADS = 4
ATTN_HEAD_DIM = 64
ATTN_BLOCK = 128
CONV_WIDTH = 31
SSM_HEAD_DIM = 64
SSM_GROUPS = 8
SSM_STATE = 128
SSM_CONV = 4
SSM_CHUNK = 128

LANES = 128
SUBLANES = 8
VMEM_LIMIT_BYTES = 56 * 1024 * 1024


def _params(n_grid_dims):
    return pltpu.CompilerParams(
        dimension_semantics=("arbitrary",) * n_grid_dims,
        vmem_limit_bytes=VMEM_LIMIT_BYTES)


def _const_spec(shape):
    nd = len(shape)
    return pl.BlockSpec(shape, lambda *_: (0,) * nd, pipeline_mode=pl.Buffered(1))


def _rms(x, g):
    return x * lax.rsqrt(jnp.mean(x * x, axis=-1, keepdims=True) + EPS) * g


def _dot(a, b):
    return jnp.dot(a, b, preferred_element_type=F32)


def _dot_nt(a, b):
    return lax.dot_general(a, b, (((1,), (1,)), ((), ())), preferred_element_type=F32)


def _left_half():
    return lax.broadcasted_iota(jnp.int32, (1, LANES), 1) < (LANES // 2)


def _mlp_ple_kernel(h_ref, p_ref, g1_ref, g2_ref, w1_ref, w2_ref, wg_ref, wp_ref, o_ref, *,
                    hidden_chunk):
    x = h_ref[...]
    u = _rms(x, g1_ref[...]).astype(BF16)
    acc = jnp.zeros_like(x)
    for c in range(0, w1_ref.shape[1], hidden_chunk):
        a = _dot(u, w1_ref[:, c:c + hidden_chunk])
        a = jnp.square(jnp.maximum(a, 0.0)).astype(BF16)
        acc = acc + _dot(a, w2_ref[c:c + hidden_chunk, :])
    h2 = x + acc
    u2 = _rms(h2, g2_ref[...]).astype(BF16)
    gate = jax.nn.sigmoid(_dot(u2, wg_ref[...]))
    proj = _dot(p_ref[...].astype(BF16), wp_ref[...])
    o_ref[...] = h2 + gate * proj


def _layer_spec(stack, layer):
    return pl.BlockSpec((None,) + stack.shape[1:], lambda *_: (layer, 0, 0), pipeline_mode=pl.Buffered(1))


def _mlp_ple(h, p, layer, g1, g2, w1, w2, wg, wp, *, tm=1024, hidden_chunk=512):
    t, d = h.shape
    tm = min(tm, t)
    row = lambda i: (i, 0)
    return pl.pallas_call(
        functools.partial(_mlp_ple_kernel, hidden_chunk=hidden_chunk),
        grid=(t // tm,),
        in_specs=[pl.BlockSpec((tm, d), row),
                  pl.BlockSpec((None, tm, p.shape[2]), lambda i: (layer, i, 0)),
                  _const_spec(g1.shape), _const_spec(g2.shape), _layer_spec(w1, layer),
                  _layer_spec(w2, layer), _layer_spec(wg, layer), _layer_spec(wp, layer)],
        out_specs=pl.BlockSpec((tm, d), row),
        out_shape=jax.ShapeDtypeStruct((t, d), F32),
        compiler_params=_params(1),
        name="mlp_ple",
    )(h, p, g1, g2, w1, w2, wg, wp)


def _qkv_kernel(h_ref, g_ref, w_ref, gq_ref, gk_ref, seg_ref, q_ref, ke_ref, ko_ref, v2_ref, *, panel):
    hd = ATTN_HEAD_DIM
    nq = q_ref.shape[1]
    nk = ATTN_KV_HEADS * hd
    u = _rms(h_ref[...], g_ref[...]).astype(BF16)
    left = _left_half()

    def pair_norm(t, g):
        sq = t * t
        sq_hi = sq.astype(BF16)
        sq_lo = (sq - sq_hi.astype(F32)).astype(BF16)
        ssq = _dot(jnp.concatenate([sq_hi, sq_lo], axis=1), seg_ref[...])
        return t * lax.rsqrt(ssq * (1.0 / hd) + EPS) * g

    def project(c0):
        return _dot(u, w_ref[:, c0:c0 + panel])

    def finish(c0, acc):
        for t0 in range(0, panel, LANES):
            c = c0 + t0
            t = acc[:, t0:t0 + LANES]
            if c < nq:
                q_ref[:, c:c + LANES] = pair_norm(t, gq_ref[...]).astype(BF16)
                continue
            j = (c - nq) % nk // LANES
            first, second = 2 * j * LANES, (2 * j + 1) * LANES
            if c < nq + nk:
                t = pair_norm(t, gk_ref[...])
                r = pltpu.roll(t, hd, axis=1)
                zero = jnp.zeros_like(t)
                ke_ref[:, first:first + LANES] = jnp.where(left, t, zero).astype(BF16)
                ke_ref[:, second:second + LANES] = jnp.where(left, r, zero).astype(BF16)
                ko_ref[:, first:first + LANES] = jnp.where(left, zero, r).astype(BF16)
                ko_ref[:, second:second + LANES] = jnp.where(left, zero, t).astype(BF16)
            else:
                r = pltpu.roll(t, hd, axis=1)
                v2_ref[:, first:first + LANES] = jnp.where(left, t, r).astype(BF16)
                v2_ref[:, second:second + LANES] = jnp.where(left, r, t).astype(BF16)

    starts = list(range(0, w_ref.shape[1], panel))
    acc = project(starts[0])
    for idx, c0 in enumerate(starts):
        nxt = project(starts[idx + 1]) if idx + 1 < len(starts) else None
        finish(c0, acc)
        acc = nxt


def _attn_core(sink_ref, bias_ref, h_ref, q_ref, ke_ref, ko_ref, v_ref, wo_ref, o_ref, oacc_ref):
    blk = ATTN_BLOCK
    group = ATTN_HEADS // ATTN_KV_HEADS
    i = pl.program_id(1)
    tq = q_ref.shape[0]

    row = lax.broadcasted_iota(jnp.int32, (blk, blk), 0)
    col = lax.broadcasted_iota(jnp.int32, (blk, blk), 1)
    lower = col <= row
    no_prev = (col - row) > jnp.where(i > 0, blk, 0)
    left = _left_half()

    def scores(b0, kv):
        lanes = slice(kv * LANES, (kv + 1) * LANES)
        qt = jnp.concatenate(
            [q_ref[b0:b0 + blk, (2 * kv) * LANES:(2 * kv + 1) * LANES],
             q_ref[b0:b0 + blk, (2 * kv + 1) * LANES:(2 * kv + 2) * LANES]], axis=0)
        return (_dot_nt(qt, ke_ref[b0:b0 + 2 * blk, lanes]),
                _dot_nt(qt, ko_ref[b0:b0 + 2 * blk, lanes]))

    items = [(b0, kv) for b0 in range(0, tq, blk) for kv in range(ATTN_KV_HEADS)]
    ahead = scores(*items[0])
    for idx, (b0, kv) in enumerate(items):
        lanes = slice(kv * LANES, (kv + 1) * LANES)
        s_even, s_odd = ahead
        if idx + 1 < len(items):
            ahead = scores(*items[idx + 1])
        probs, r_den = [], []
        for g in range(group):
            src = s_even if g % 2 == 0 else s_odd
            r0 = (g // 2) * blk
            hh = kv * group + g
            s = jnp.where(lower, src[r0:r0 + blk, blk:], src[r0:r0 + blk, :blk]) + bias_ref[hh]
            if b0 == 0:
                s = jnp.where(no_prev, -jnp.inf, s)
            sink = sink_ref[hh]
            m = jnp.maximum(jnp.max(s, axis=-1, keepdims=True), sink)
            e = jnp.exp(s - m)
            r_den.append(1.0 / (jnp.sum(e, axis=-1, keepdims=True) + jnp.exp(sink - m)))
            probs.append(jnp.concatenate([jnp.where(lower, 0.0, e), jnp.where(lower, e, 0.0)],
                                         axis=1).astype(BF16))
        o4 = _dot(jnp.concatenate(probs, axis=0), v_ref[b0:b0 + 2 * blk, lanes])
        for pair in range(group // 2):
            ga, gb = 2 * pair, 2 * pair + 1
            oa = o4[ga * blk:(ga + 1) * blk, :] * r_den[ga]
            ob = o4[gb * blk:(gb + 1) * blk, :] * r_den[gb]
            c0 = (2 * kv + pair) * LANES
            oacc_ref[b0:b0 + blk, c0:c0 + LANES] = jnp.where(left, oa, ob).astype(BF16)
    o_ref[...] = h_ref[...] + _dot(oacc_ref[...], wo_ref[...])


def _alibi_bias():
    blk = ATTN_BLOCK
    row = jnp.arange(blk, dtype=jnp.int32)[:, None]
    col = jnp.arange(blk, dtype=jnp.int32)[None, :]
    rel = jnp.where(col <= row, row - col, row - col + blk).astype(F32)
    slopes = jnp.exp2(-8.0 * (jnp.arange(ATTN_HEADS, dtype=F32) + 1.0) / ATTN_HEADS)
    return -(slopes[:, None, None] * rel[None])


def _attention_layer(h, batch, seq, g, wqkv, gq, gk, sinks, wo, *, tm=512):
    t, d = h.shape
    hd, blk = ATTN_HEAD_DIM, ATTN_BLOCK
    nq, nkp = ATTN_HEADS * hd, ATTN_KV_HEADS * LANES
    tm = min(tm, seq)
    lane_head = jnp.arange(LANES) // hd
    seg = jnp.tile((lane_head[:, None] == lane_head[None, :]).astype(BF16), (2, 1))
    bias = _alibi_bias()
    nt = seq // tm
    cur = lambda b, i: (b * nt + i, 0)

    def fused(sink_ref, bias_ref, h_ref, g_ref, wqkv_ref, gq_ref, gk_ref, seg_ref, wo_ref, o_ref,
              q_ref, ke_ref, ko_ref, v_ref, oacc_ref):
        i = pl.program_id(1)
        kv_bufs = (ke_ref, ko_ref, v_ref)

        @pl.when(i == 0)
        def _():
            for buf in kv_bufs:
                buf[0:blk, :] = jnp.zeros((blk, nkp), BF16)

        _qkv_kernel(h_ref, g_ref, wqkv_ref, gq_ref, gk_ref, seg_ref, q_ref,
                    *(buf.at[blk:, :] for buf in kv_bufs), panel=2 * LANES)
        _attn_core(sink_ref, bias_ref, h_ref, q_ref, ke_ref, ko_ref, v_ref, wo_ref, o_ref, oacc_ref)
        for buf in kv_bufs:
            buf[0:blk, :] = buf[tm:tm + blk, :]

    return pl.pallas_call(
        fused,
        grid=(batch, nt),
        in_specs=[pl.BlockSpec(memory_space=pltpu.SMEM), _const_spec(bias.shape),
                  pl.BlockSpec((tm, d), cur), _const_spec(g.shape), _const_spec(wqkv.shape),
                  _const_spec(gq.shape), _const_spec(gk.shape), _const_spec(seg.shape),
                  _const_spec(wo.shape)],
        out_specs=pl.BlockSpec((tm, d), cur),
        out_shape=jax.ShapeDtypeStruct((t, d), F32),
        scratch_shapes=[pltpu.VMEM((tm, nq), BF16), pltpu.VMEM((tm + blk, nkp), BF16),
                        pltpu.VMEM((tm + blk, nkp), BF16), pltpu.VMEM((tm + blk, nkp), BF16),
                        pltpu.VMEM((tm, nq), BF16)],
        compiler_params=_params(2),
        name="attention",
    )(sinks, bias, h, g, wqkv, gq, gk, seg, wo)


CONV_HALO = 32


def _conformer_kernel(h_ref, g_ref, w1_ref, b1_ref, wdw_ref, bdw_ref, lg_ref, lb_ref, w2_ref, b2_ref,
                      o_ref, ycat_ref, ysh_ref, conv_ref, *, row_chunk, col_chunk):
    i = pl.program_id(1)
    n_streams, tm, d = h_ref.shape
    halo = CONV_HALO
    span = tm + halo - SUBLANES
    first = halo - (CONV_WIDTH - 1)

    @pl.when(i == 0)
    def _():
        ycat_ref[:, 0:halo, :] = jnp.zeros((n_streams, halo, d), F32)

    for s in range(n_streams):
        u = _rms(h_ref[s], g_ref[...]).astype(BF16)
        a = _dot(u, w1_ref[...]) + b1_ref[...]
        ycat_ref[s, halo:, :] = a[:, :d] * jax.nn.sigmoid(a[:, d:])

    for s in range(n_streams):
        for ph in range(1, SUBLANES):
            ysh_ref[s, ph - 1] = ycat_ref[s, ph:ph + span, :]
        for r0 in range(0, tm, row_chunk):
            for c0 in range(0, d, col_chunk):
                cols = slice(c0, c0 + col_chunk)
                acc = jnp.zeros((row_chunk, col_chunk), F32)
                for k in range(CONV_WIDTH):
                    phase = (first + k) % SUBLANES
                    base = r0 + first + k - phase
                    if phase == 0:
                        tap = ycat_ref[s, base:base + row_chunk, cols]
                    else:
                        tap = ysh_ref[s, phase - 1, base:base + row_chunk, cols]
                    acc = acc + wdw_ref[k:k + 1, cols] * tap
                conv_ref[s, r0:r0 + row_chunk, cols] = acc + bdw_ref[:, cols]
        ycat_ref[s, 0:halo, :] = ycat_ref[s, tm:tm + halo, :]

    for s in range(n_streams):
        c = conv_ref[s]
        mu = jnp.mean(c, axis=-1, keepdims=True)
        cc = c - mu
        var = jnp.mean(cc * cc, axis=-1, keepdims=True)
        y = cc * lax.rsqrt(var + EPS) * lg_ref[...] + lb_ref[...]
        y = y * jax.nn.sigmoid(y)
        o_ref[s] = h_ref[s] + _dot(y.astype(BF16), w2_ref[...]) + b2_ref[...]


def _conformer_layer(h, batch, seq, g, w1, b1, wdw, bdw, lg, lb, w2, b2, *, tm=256, n_streams=2):
    t, d = h.shape
    tm = min(tm, seq)
    n_streams = min(n_streams, batch)
    cur = lambda b, i: (b, i, 0)
    out = pl.pallas_call(
        functools.partial(_conformer_kernel, row_chunk=64, col_chunk=256),
        grid=(batch // n_streams, seq // tm),
        in_specs=[pl.BlockSpec((n_streams, tm, d), cur), _const_spec(g.shape), _const_spec(w1.shape),
                  _const_spec(b1.shape), _const_spec(wdw.shape), _const_spec(bdw.shape),
                  _const_spec(lg.shape), _const_spec(lb.shape), _const_spec(w2.shape),
                  _const_spec(b2.shape)],
        out_specs=pl.BlockSpec((n_streams, tm, d), cur),
        out_shape=jax.ShapeDtypeStruct((batch, seq, d), F32),
        scratch_shapes=[pltpu.VMEM((n_streams, tm + CONV_HALO, d), F32),
                        pltpu.VMEM((n_streams, SUBLANES - 1, tm + CONV_HALO - SUBLANES, d), F32),
                        pltpu.VMEM((n_streams, tm, d), F32)],
        compiler_params=_params(2),
        name="conformer",
    )(h.reshape(batch, seq, d), g, w1, b1, wdw, bdw, lg, lb, w2, b2)
    return out.reshape(t, d)


def _mamba_in_kernel(h_ref, g_ref, win_ref, wc_ref, bc_ref, dtb_ref, alog_ref,
                     expand_ref, z_ref, xs_ref, xdt_ref, bt_ref, c_ref, cs_ref, cst_ref,
                     xcat_ref, hist_ref, bact_ref, dte_ref, *, proj_chunk, row_chunk, col_chunk):
    i = pl.program_id(1)
    tm = h_ref.shape[0]
    di = xs_ref.shape[1]
    gn = c_ref.shape[1]
    L = SSM_CHUNK
    halo = SUBLANES
    n_xbc = di + 2 * gn
    u = _rms(h_ref[...], g_ref[...]).astype(BF16)
    dt = jax.nn.softplus(_dot(u, win_ref[:, di + n_xbc:]) + dtb_ref[...])

    @pl.when(i == 0)
    def _():
        hist_ref[...] = jnp.zeros(hist_ref.shape, F32)

    def project_z(q):
        z_ref[:, q * proj_chunk:(q + 1) * proj_chunk] = _dot(u, win_ref[:, q * proj_chunk:(q + 1) * proj_chunk])

    def project_xbc(panel, slot):
        p0 = di + panel * proj_chunk
        xcat_ref[slot, 0:halo, :] = hist_ref[panel]
        xcat_ref[slot, halo:, :] = _dot(u, win_ref[:, p0:p0 + proj_chunk])

    def time_sums():
        r_i = lax.broadcasted_iota(jnp.int32, (L, L), 0)
        c_i = lax.broadcasted_iota(jnp.int32, (L, L), 1)
        tri = (r_i >= c_i).astype(F32)
        a_row = -jnp.exp(alog_ref[...]) * LOG2_E
        for c in range(tm // L):
            cs = jnp.dot(tri, dt[c * L:(c + 1) * L, :] * a_row, precision=lax.Precision.HIGHEST,
                         preferred_element_type=F32)
            cs_ref[c * L:(c + 1) * L, :] = cs
            cst_ref[c] = cs.T

    dt_hi = dt.astype(BF16)
    dt_lo = (dt - dt_hi.astype(F32)).astype(BF16)
    dt_terms = jnp.concatenate([dt_hi, dt_lo], axis=1)

    def expand_dt(panel):
        p0 = panel * proj_chunk
        dte_ref[...] = _dot(dt_terms, expand_ref[:, p0:p0 + proj_chunk])

    def conv_silu(panel, slot):
        p0 = panel * proj_chunk
        hist_ref[panel] = xcat_ref[slot, tm:tm + halo, :]
        for r0 in range(0, tm, row_chunk):
            for c0 in range(p0, p0 + proj_chunk, col_chunk):
                rows = slice(r0, r0 + row_chunk)
                cols = slice(c0, c0 + col_chunk)
                xb = xcat_ref[slot, r0:r0 + halo + row_chunk, c0 - p0:c0 - p0 + col_chunk]
                x1 = pltpu.roll(xb, 1, axis=0)
                near = wc_ref[3:4, cols] * xb + wc_ref[2:3, cols] * x1
                far = wc_ref[1:2, cols] * xb + wc_ref[0:1, cols] * x1
                acc = near[halo:, :] + pltpu.roll(far, 2, axis=0)[halo:, :] + bc_ref[:, cols]
                act = acc * jax.nn.sigmoid(acc)
                if c0 < di:
                    xs_ref[rows, cols] = act
                    xdt_ref[rows, cols] = (act * dte_ref[rows, c0 - p0:c0 - p0 + col_chunk]).astype(BF16)
                elif c0 < di + gn:
                    bact_ref[rows, c0 - di:c0 - di + col_chunk] = act
                else:
                    c_ref[rows, c0 - di - gn:c0 - di - gn + col_chunk] = act.astype(BF16)

    def transpose_b():
        for c in range(tm // L):
            for g in range(SSM_GROUPS):
                n0 = g * SSM_STATE
                bt_ref[c, n0:n0 + SSM_STATE, :] = bact_ref[c * L:(c + 1) * L, n0:n0 + SSM_STATE].T.astype(BF16)

    n_x, n_b = di // proj_chunk, gn // proj_chunk
    order = list(range(n_x, n_xbc // proj_chunk)) + list(range(n_x))
    project_xbc(order[0], 0)
    for step, panel in enumerate(order):
        if step + 1 < len(order):
            project_xbc(order[step + 1], (step + 1) % 2)
        if step < n_x:
            project_z(step)
        if step == 0:
            time_sums()
        if panel < n_x:
            expand_dt(panel)
        conv_silu(panel, step % 2)
        if panel == n_x + n_b - 1:
            transpose_b()


def _ssd_kernel(h_ref, z_ref, xs_ref, xdt_ref, bt_ref, c_ref, cs_ref, cst_ref, dexp_ref, ng_ref,
                wout_ref, o_ref, state_ref, y_ref, yn_ref):
    i = pl.program_id(1)
    tm = h_ref.shape[0]
    L = SSM_CHUNK
    gw = 4 * SSM_HEAD_DIM

    @pl.when(i == 0)
    def _():
        state_ref[...] = jnp.zeros(state_ref.shape, F32)

    r_i = lax.broadcasted_iota(jnp.int32, (L, L), 0)
    c_i = lax.broadcasted_iota(jnp.int32, (L, L), 1)
    causal = r_i >= c_i
    left = _left_half()

    def chunk(c, carry):
        r0 = pl.multiple_of(c * L, L)
        cs = cs_ref[pl.ds(r0, L), :]
        cs_t = cst_ref[c]
        cbs, y_offs = [], []
        for g in range(SSM_GROUPS):
            n0 = g * SSM_STATE
            cg = c_ref[pl.ds(r0, L), n0:n0 + SSM_STATE]
            cbs.append(_dot(cg, bt_ref[c, n0:n0 + SSM_STATE, :]))
            y_offs.append(_dot(cg, state_ref[g].astype(BF16)))
        xws, carry_decay = [], []
        for g in range(SSM_GROUPS):
            cb = cbs[g]
            xg = xdt_ref[pl.ds(r0, L), g * gw:(g + 1) * gw]
            y_tiles, ecs_tiles, dec_tiles = [], [], []
            for pair in range(2):
                xt = xg[:, pair * LANES:(pair + 1) * LANES]
                ms, bcs = [], []
                for half in range(2):
                    hh = 4 * g + 2 * pair + half
                    bc = jnp.broadcast_to(cs[:, hh:hh + 1], (L, L))
                    lm = jnp.exp2(jnp.where(causal, bc - cs_t[hh:hh + 1, :], -jnp.inf))
                    ms.append((cb * lm).astype(BF16))
                    bcs.append(bc)
                prod = _dot(jnp.concatenate(ms, axis=0), xt)
                y_tiles.append(jnp.where(left, prod[:L, :], prod[L:, :]))
                cs_e = jnp.where(left, bcs[0], bcs[1])
                ecs_tiles.append(jnp.exp2(cs_e))
                dec_tiles.append(jnp.exp2(cs_e[L - 1:L, :] - cs_e))
            ecs_e = jnp.concatenate(ecs_tiles, axis=1)
            y_off = y_offs[g] * ecs_e
            xws.append((xg.astype(F32) * jnp.concatenate(dec_tiles, axis=1)).astype(BF16))
            carry_decay.append(ecs_e[L - 1:L, :])
            cols = slice(g * gw, (g + 1) * gw)
            zz = z_ref[pl.ds(r0, L), cols]
            yg = jnp.concatenate(y_tiles, axis=1) + y_off + xs_ref[pl.ds(r0, L), cols] * dexp_ref[:, cols]
            yg = yg * (zz * jax.nn.sigmoid(zz))
            y_ref[pl.ds(r0, L), cols] = yg
            sq = yg * yg
            sq = sq[:, :LANES] + sq[:, LANES:]
            ssq = sq if g == 0 else ssq + sq
        for g in range(SSM_GROUPS):
            n0 = g * SSM_STATE
            state_ref[g] = state_ref[g] * carry_decay[g] + _dot(bt_ref[c, n0:n0 + SSM_STATE, :], xws[g])
        inv = lax.rsqrt(jnp.sum(ssq, axis=-1, keepdims=True) * (1.0 / y_ref.shape[1]) + EPS)
        yn_ref[pl.ds(r0, L), :] = (y_ref[pl.ds(r0, L), :] * inv * ng_ref[...]).astype(BF16)
        return carry

    lax.fori_loop(0, tm // L, chunk, 0)
    o_ref[...] = h_ref[...] + _dot(yn_ref[...], wout_ref[...])


def _mamba_layer(h, batch, seq, g, win, wc, bc, dtb, alog, dexp, ng, wout, *,
                 tm=512, proj_chunk=512):
    t, d = h.shape
    di = wout.shape[0]
    gn = SSM_GROUPS * SSM_STATE
    L = SSM_CHUNK
    tm = min(tm, seq)
    nt = seq // tm
    cur = lambda b, i: (b * nt + i, 0)
    expand = jnp.tile(jnp.repeat(jnp.eye(LANES, di // SSM_HEAD_DIM, dtype=BF16), SSM_HEAD_DIM, axis=1),
                      (2, 1))
    def fused(h_ref, g_ref, win_ref, wc_ref, bc_ref, dtb_ref, alog_ref, expand_ref, dexp_ref, ng_ref,
              wout_ref, o_ref, z_ref, xs_ref, xdt_ref, bt_ref, c_ref, cs_ref, cst_ref,
              xcat_ref, hist_ref, dte_ref, state_ref, y_ref, yn_ref):
        bact_ref = y_ref.at[:, 0:gn]
        _mamba_in_kernel(h_ref, g_ref, win_ref, wc_ref, bc_ref, dtb_ref, alog_ref, expand_ref,
                         z_ref, xs_ref, xdt_ref, bt_ref, c_ref, cs_ref, cst_ref,
                         xcat_ref, hist_ref, bact_ref, dte_ref,
                         proj_chunk=proj_chunk, row_chunk=64, col_chunk=256)
        _ssd_kernel(h_ref, z_ref, xs_ref, xdt_ref, bt_ref, c_ref, cs_ref, cst_ref, dexp_ref, ng_ref,
                    wout_ref, o_ref, state_ref, y_ref, yn_ref)

    return pl.pallas_call(
        fused,
        grid=(batch, nt),
        in_specs=[pl.BlockSpec((tm, d), cur), _const_spec(g.shape), _const_spec(win.shape),
                  _const_spec(wc.shape), _const_spec(bc.shape), _const_spec(dtb.shape),
                  _const_spec(alog.shape), _const_spec(expand.shape), _const_spec(dexp.shape),
                  _const_spec(ng.shape), _const_spec(wout.shape)],
        out_specs=pl.BlockSpec((tm, d), cur),
        out_shape=jax.ShapeDtypeStruct((t, d), F32),
        scratch_shapes=[
            pltpu.VMEM((tm, di), F32), pltpu.VMEM((tm, di), F32), pltpu.VMEM((tm, di), BF16),
            pltpu.VMEM((tm // L, gn, L), BF16), pltpu.VMEM((tm, gn), BF16),
            pltpu.VMEM((tm, LANES), F32), pltpu.VMEM((tm // L, LANES, L), F32),
            pltpu.VMEM((2, tm + SUBLANES, proj_chunk), F32),
            pltpu.VMEM(((di + 2 * gn) // proj_chunk, SUBLANES, proj_chunk), F32),
            pltpu.VMEM((tm, proj_chunk), F32),
            pltpu.VMEM((SSM_GROUPS, SSM_STATE, 4 * SSM_HEAD_DIM), F32),
            pltpu.VMEM((tm, di), F32), pltpu.VMEM((tm, di), BF16)],
        compiler_params=_params(2),
        name="mamba",
    )(h, g, win, wc, bc, dtb, alog, expand, dexp, ng, wout)


def _row(v):
    return v.reshape(1, -1).astype(F32)


def _pad_cols(a, n):
    return jnp.pad(a, ((0, 0), (0, n - a.shape[1])))


def kernel(x, p, mix_norm_g, mlp_norm_g, ple_norm_g, a_wqkv, a_q_norm_g, a_k_norm_g, a_sinks, a_wo, b_w_pw1, b_b_pw1, b_w_dw, b_b_dw, b_ln_g, b_ln_b, b_w_pw2, b_b_pw2, c_w_in, c_w_conv, c_b_conv, c_dt_bias, c_A_log, c_D, c_norm_g, c_w_out, m_w1, m_w2, ple_w_proj, ple_w_gate):
    batch, seq, d = x.shape
    depth = p.shape[0]
    t = batch * seq
    h = x.reshape(t, d)
    p = p.reshape(depth, t, -1)
    w1_all, w2_all = m_w1.astype(BF16), m_w2.astype(BF16)
    wg_all, wp_all = ple_w_gate.astype(BF16), ple_w_proj.astype(BF16)
    for i in range(depth):
        kind, j = i % N_MIXERS, i // N_MIXERS
        g = _row(mix_norm_g[i])
        if kind == 0:
            scale = ATTN_HEAD_DIM ** -0.5
            gq = _row(jnp.tile(a_q_norm_g[j] * scale, 2))
            gk = _row(jnp.tile(a_k_norm_g[j], 2))
            h = _attention_layer(h, batch, seq, g, a_wqkv[j].astype(BF16), gq, gk,
                                 a_sinks[j].astype(F32), a_wo[j].astype(BF16))
        elif kind == 1:
            wdw = jnp.pad(b_w_dw[j], ((0, CONV_HALO - CONV_WIDTH), (0, 0)))
            h = _conformer_layer(h, batch, seq, g, b_w_pw1[j].astype(BF16), _row(b_b_pw1[j]),
                                 wdw, _row(b_b_dw[j]), _row(b_ln_g[j]), _row(b_ln_b[j]),
                                 b_w_pw2[j].astype(BF16), _row(b_b_pw2[j]))
        else:
            di = c_w_out.shape[1]
            gn = SSM_GROUPS * SSM_STATE
            win = _pad_cols(c_w_in[j], 2 * di + 2 * gn + LANES).astype(BF16)
            wc = jnp.pad(c_w_conv[j], ((0, SUBLANES - SSM_CONV), (0, 0)))
            dtb = _pad_cols(_row(c_dt_bias[j]), LANES)
            alog = _pad_cols(_row(c_A_log[j]), LANES)
            dexp = _row(jnp.repeat(c_D[j], SSM_HEAD_DIM))
            h = _mamba_layer(h, batch, seq, g, win, wc, _row(c_b_conv[j]), dtb, alog,
                             dexp, _row(c_norm_g[j]), c_w_out[j].astype(BF16))
        h = _mlp_ple(h, p, i, _row(mlp_norm_g[i]), _row(ple_norm_g[i]), w1_all, w2_all, wg_all, wp_all)
    return h.reshape(batch, seq, d)
```

```python
import functools

import jax
import jax.numpy as jnp
from jax import lax
from jax.experimental import pallas as pl
from jax.experimental.pallas import tpu as pltpu

F32 = jnp.float32
BF16 = jnp.bfloat16
EPS = 1e-6
LOG2_E = 1.4426950408889634

N_MIXERS = 3
ATTN_HEADS = 16
ATTN_KV_HEADS = 4
ATTN_HEAD_DIM = 64
ATTN_BLOCK = 128
CONV_WIDTH = 31
SSM_HEAD_DIM = 64
SSM_GROUPS = 8
SSM_STATE = 128
SSM_CONV = 4
SSM_CHUNK = 128

LANES = 128
SUBLANES = 8
VMEM_LIMIT_BYTES = 56 * 1024 * 1024


def _params(n_grid_dims):
    return pltpu.CompilerParams(
        dimension_semantics=("arbitrary",) * n_grid_dims,
        vmem_limit_bytes=VMEM_LIMIT_BYTES)


def _const_spec(shape):
    nd = len(shape)
    return pl.BlockSpec(shape, lambda *_: (0,) * nd, pipeline_mode=pl.Buffered(1))


def _rms(x, g):
    return x * lax.rsqrt(jnp.mean(x * x, axis=-1, keepdims=True) + EPS) * g


def _dot(a, b):
    return jnp.dot(a, b, preferred_element_type=F32)


def _dot_nt(a, b):
    return lax.dot_general(a, b, (((1,), (1,)), ((), ())), preferred_element_type=F32)


def _left_half():
    return lax.broadcasted_iota(jnp.int32, (1, LANES), 1) < (LANES // 2)


def _mlp_ple_kernel(h_ref, p_ref, g1_ref, g2_ref, w1_ref, w2_ref, wg_ref, wp_ref, o_ref, *,
                    hidden_chunk):
    x = h_ref[...]
    u = _rms(x, g1_ref[...]).astype(BF16)
    acc = jnp.zeros_like(x)
    for c in range(0, w1_ref.shape[1], hidden_chunk):
        a = _dot(u, w1_ref[:, c:c + hidden_chunk])
        a = jnp.square(jnp.maximum(a, 0.0)).astype(BF16)
        acc = acc + _dot(a, w2_ref[c:c + hidden_chunk, :])
    h2 = x + acc
    u2 = _rms(h2, g2_ref[...]).astype(BF16)
    gate = jax.nn.sigmoid(_dot(u2, wg_ref[...]))
    proj = _dot(p_ref[...].astype(BF16), wp_ref[...])
    o_ref[...] = h2 + gate * proj


def _layer_spec(stack, layer):
    return pl.BlockSpec((None,) + stack.shape[1:], lambda *_: (layer, 0, 0), pipeline_mode=pl.Buffered(1))


def _mlp_ple(h, p, layer, g1, g2, w1, w2, wg, wp, *, tm=1024, hidden_chunk=512):
    t, d = h.shape
    tm = min(tm, t)
    row = lambda i: (i, 0)
    return pl.pallas_call(
        functools.partial(_mlp_ple_kernel, hidden_chunk=hidden_chunk),
        grid=(t // tm,),
        in_specs=[pl.BlockSpec((tm, d), row),
                  pl.BlockSpec((None, tm, p.shape[2]), lambda i: (layer, i, 0)),
                  _const_spec(g1.shape), _const_spec(g2.shape), _layer_spec(w1, layer),
                  _layer_spec(w2, layer), _layer_spec(wg, layer), _layer_spec(wp, layer)],
        out_specs=pl.BlockSpec((tm, d), row),
        out_shape=jax.ShapeDtypeStruct((t, d), F32),
        compiler_params=_params(1),
        name="mlp_ple",
    )(h, p, g1, g2, w1, w2, wg, wp)


def _qkv_kernel(h_ref, g_ref, w_ref, gq_ref, gk_ref, seg_ref, q_ref, ke_ref, ko_ref, v2_ref, *, panel):
    hd = ATTN_HEAD_DIM
    nq = q_ref.shape[1]
    nk = ATTN_KV_HEADS * hd
    u = _rms(h_ref[...], g_ref[...]).astype(BF16)
    left = _left_half()

    def pair_norm(t, g):
        sq = t * t
        sq_hi = sq.astype(BF16)
        sq_lo = (sq - sq_hi.astype(F32)).astype(BF16)
        ssq = _dot(jnp.concatenate([sq_hi, sq_lo], axis=1), seg_ref[...])
        return t * lax.rsqrt(ssq * (1.0 / hd) + EPS) * g

    def project(c0):
        return _dot(u, w_ref[:, c0:c0 + panel])

    def finish(c0, acc):
        for t0 in range(0, panel, LANES):
            c = c0 + t0
            t = acc[:, t0:t0 + LANES]
            if c < nq:
                q_ref[:, c:c + LANES] = pair_norm(t, gq_ref[...]).astype(BF16)
                continue
            j = (c - nq) % nk // LANES
            first, second = 2 * j * LANES, (2 * j + 1) * LANES
            if c < nq + nk:
                t = pair_norm(t, gk_ref[...])
                r = pltpu.roll(t, hd, axis=1)
                zero = jnp.zeros_like(t)
                ke_ref[:, first:first + LANES] = jnp.where(left, t, zero).astype(BF16)
                ke_ref[:, second:second + LANES] = jnp.where(left, r, zero).astype(BF16)
                ko_ref[:, first:first + LANES] = jnp.where(left, zero, r).astype(BF16)
                ko_ref[:, second:second + LANES] = jnp.where(left, zero, t).astype(BF16)
            else:
                r = pltpu.roll(t, hd, axis=1)
                v2_ref[:, first:first + LANES] = jnp.where(left, t, r).astype(BF16)
                v2_ref[:, second:second + LANES] = jnp.where(left, r, t).astype(BF16)

    starts = list(range(0, w_ref.shape[1], panel))
    acc = project(starts[0])
    for idx, c0 in enumerate(starts):
        nxt = project(starts[idx + 1]) if idx + 1 < len(starts) else None
        finish(c0, acc)
        acc = nxt


def _attn_core(sink_ref, bias_ref, h_ref, q_ref, ke_ref, ko_ref, v_ref, wo_ref, o_ref, oacc_ref):
    blk = ATTN_BLOCK
    group = ATTN_HEADS // ATTN_KV_HEADS
    i = pl.program_id(1)
    tq = q_ref.shape[0]

    row = lax.broadcasted_iota(jnp.int32, (blk, blk), 0)
    col = lax.broadcasted_iota(jnp.int32, (blk, blk), 1)
    lower = col <= row
    no_prev = (col - row) > jnp.where(i > 0, blk, 0)
    left = _left_half()

    def scores(b0, kv):
        lanes = slice(kv * LANES, (kv + 1) * LANES)
        qt = jnp.concatenate(
            [q_ref[b0:b0 + blk, (2 * kv) * LANES:(2 * kv + 1) * LANES],
             q_ref[b0:b0 + blk, (2 * kv + 1) * LANES:(2 * kv + 2) * LANES]], axis=0)
        return (_dot_nt(qt, ke_ref[b0:b0 + 2 * blk, lanes]),
                _dot_nt(qt, ko_ref[b0:b0 + 2 * blk, lanes]))

    items = [(b0, kv) for b0 in range(0, tq, blk) for kv in range(ATTN_KV_HEADS)]
    ahead = scores(*items[0])
    for idx, (b0, kv) in enumerate(items):
        lanes = slice(kv * LANES, (kv + 1) * LANES)
        s_even, s_odd = ahead
        if idx + 1 < len(items):
            ahead = scores(*items[idx + 1])
        probs, r_den = [], []
        for g in range(group):
            src = s_even if g % 2 == 0 else s_odd
            r0 = (g // 2) * blk
            hh = kv * group + g
            s = jnp.where(lower, src[r0:r0 + blk, blk:], src[r0:r0 + blk, :blk]) + bias_ref[hh]
            if b0 == 0:
                s = jnp.where(no_prev, -jnp.inf, s)
            sink = sink_ref[hh]
            m = jnp.maximum(jnp.max(s, axis=-1, keepdims=True), sink)
            e = jnp.exp(s - m)
            r_den.append(1.0 / (jnp.sum(e, axis=-1, keepdims=True) + jnp.exp(sink - m)))
            probs.append(jnp.concatenate([jnp.where(lower, 0.0, e), jnp.where(lower, e, 0.0)],
                                         axis=1).astype(BF16))
        o4 = _dot(jnp.concatenate(probs, axis=0), v_ref[b0:b0 + 2 * blk, lanes])
        for pair in range(group // 2):
            ga, gb = 2 * pair, 2 * pair + 1
            oa = o4[ga * blk:(ga + 1) * blk, :] * r_den[ga]
            ob = o4[gb * blk:(gb + 1) * blk, :] * r_den[gb]
            c0 = (2 * kv + pair) * LANES
            oacc_ref[b0:b0 + blk, c0:c0 + LANES] = jnp.where(left, oa, ob).astype(BF16)
    o_ref[...] = h_ref[...] + _dot(oacc_ref[...], wo_ref[...])


def _alibi_bias():
    blk = ATTN_BLOCK
    row = jnp.arange(blk, dtype=jnp.int32)[:, None]
    col = jnp.arange(blk, dtype=jnp.int32)[None, :]
    rel = jnp.where(col <= row, row - col, row - col + blk).astype(F32)
    slopes = jnp.exp2(-8.0 * (jnp.arange(ATTN_HEADS, dtype=F32) + 1.0) / ATTN_HEADS)
    return -(slopes[:, None, None] * rel[None])


def _attention_layer(h, batch, seq, g, wqkv, gq, gk, sinks, wo, *, tm=1024):
    t, d = h.shape
    hd, blk = ATTN_HEAD_DIM, ATTN_BLOCK
    nq, nkp = ATTN_HEADS * hd, ATTN_KV_HEADS * LANES
    tm = min(tm, seq)
    lane_head = jnp.arange(LANES) // hd
    seg = jnp.tile((lane_head[:, None] == lane_head[None, :]).astype(BF16), (2, 1))
    bias = _alibi_bias()
    nt = seq // tm
    cur = lambda b, i: (b * nt + i, 0)

    def fused(sink_ref, bias_ref, h_ref, g_ref, wqkv_ref, gq_ref, gk_ref, seg_ref, wo_ref, o_ref,
              q_ref, ke_ref, ko_ref, v_ref, oacc_ref):
        i = pl.program_id(1)
        kv_bufs = (ke_ref, ko_ref, v_ref)

        @pl.when(i == 0)
        def _():
            for buf in kv_bufs:
                buf[0:blk, :] = jnp.zeros((blk, nkp), BF16)

        _qkv_kernel(h_ref, g_ref, wqkv_ref, gq_ref, gk_ref, seg_ref, q_ref,
                    *(buf.at[blk:, :] for buf in kv_bufs), panel=2 * LANES)
        _attn_core(sink_ref, bias_ref, h_ref, q_ref, ke_ref, ko_ref, v_ref, wo_ref, o_ref, oacc_ref)
        for buf in kv_bufs:
            buf[0:blk, :] = buf[tm:tm + blk, :]

    return pl.pallas_call(
        fused,
        grid=(batch, nt),
        in_specs=[pl.BlockSpec(memory_space=pltpu.SMEM), _const_spec(bias.shape),
                  pl.BlockSpec((tm, d), cur), _const_spec(g.shape), _const_spec(wqkv.shape),
                  _const_spec(gq.shape), _const_spec(gk.shape), _const_spec(seg.shape),
                  _const_spec(wo.shape)],
        out_specs=pl.BlockSpec((tm, d), cur),
        out_shape=jax.ShapeDtypeStruct((t, d), F32),
        scratch_shapes=[pltpu.VMEM((tm, nq), BF16), pltpu.VMEM((tm + blk, nkp), BF16),
                        pltpu.VMEM((tm + blk, nkp), BF16), pltpu.VMEM((tm + blk, nkp), BF16),
                        pltpu.VMEM((tm, nq), BF16)],
        compiler_params=_params(2),
        name="attention",
    )(sinks, bias, h, g, wqkv, gq, gk, seg, wo)


CONV_HALO = 32


def _conformer_kernel(h_ref, g_ref, w1_ref, b1_ref, wdw_ref, bdw_ref, lg_ref, lb_ref, w2_ref, b2_ref,
                      o_ref, ycat_ref, ysh_ref, conv_ref, *, row_chunk, col_chunk):
    i = pl.program_id(1)
    n_streams, tm, d = h_ref.shape
    halo = CONV_HALO
    span = tm + halo - SUBLANES
    first = halo - (CONV_WIDTH - 1)

    @pl.when(i == 0)
    def _():
        ycat_ref[:, 0:halo, :] = jnp.zeros((n_streams, halo, d), F32)

    for s in range(n_streams):
        u = _rms(h_ref[s], g_ref[...]).astype(BF16)
        a = _dot(u, w1_ref[...]) + b1_ref[...]
        ycat_ref[s, halo:, :] = a[:, :d] * jax.nn.sigmoid(a[:, d:])

    for s in range(n_streams):
        for ph in range(1, SUBLANES):
            ysh_ref[s, ph - 1] = ycat_ref[s, ph:ph + span, :]
        for r0 in range(0, tm, row_chunk):
            for c0 in range(0, d, col_chunk):
                cols = slice(c0, c0 + col_chunk)
                acc = jnp.zeros((row_chunk, col_chunk), F32)
                for k in range(CONV_WIDTH):
                    phase = (first + k) % SUBLANES
                    base = r0 + first + k - phase
                    if phase == 0:
                        tap = ycat_ref[s, base:base + row_chunk, cols]
                    else:
                        tap = ysh_ref[s, phase - 1, base:base + row_chunk, cols]
                    acc = acc + wdw_ref[k:k + 1, cols] * tap
                conv_ref[s, r0:r0 + row_chunk, cols] = acc + bdw_ref[:, cols]
        ycat_ref[s, 0:halo, :] = ycat_ref[s, tm:tm + halo, :]

    for s in range(n_streams):
        c = conv_ref[s]
        mu = jnp.mean(c, axis=-1, keepdims=True)
        cc = c - mu
        var = jnp.mean(cc * cc, axis=-1, keepdims=True)
        y = cc * lax.rsqrt(var + EPS) * lg_ref[...] + lb_ref[...]
        y = y * jax.nn.sigmoid(y)
        o_ref[s] = h_ref[s] + _dot(y.astype(BF16), w2_ref[...]) + b2_ref[...]


def _conformer_layer(h, batch, seq, g, w1, b1, wdw, bdw, lg, lb, w2, b2, *, tm=256, n_streams=2):
    t, d = h.shape
    tm = min(tm, seq)
    n_streams = min(n_streams, batch)
    cur = lambda b, i: (b, i, 0)
    out = pl.pallas_call(
        functools.partial(_conformer_kernel, row_chunk=128, col_chunk=128),
        grid=(batch // n_streams, seq // tm),
        in_specs=[pl.BlockSpec((n_streams, tm, d), cur), _const_spec(g.shape), _const_spec(w1.shape),
                  _const_spec(b1.shape), _const_spec(wdw.shape), _const_spec(bdw.shape),
                  _const_spec(lg.shape), _const_spec(lb.shape), _const_spec(w2.shape),
                  _const_spec(b2.shape)],
        out_specs=pl.BlockSpec((n_streams, tm, d), cur),
        out_shape=jax.ShapeDtypeStruct((batch, seq, d), F32),
        scratch_shapes=[pltpu.VMEM((n_streams, tm + CONV_HALO, d), F32),
                        pltpu.VMEM((n_streams, SUBLANES - 1, tm + CONV_HALO - SUBLANES, d), F32),
                        pltpu.VMEM((n_streams, tm, d), F32)],
        compiler_params=_params(2),
        name="conformer",
    )(h.reshape(batch, seq, d), g, w1, b1, wdw, bdw, lg, lb, w2, b2)
    return out.reshape(t, d)


def _mamba_in_kernel(h_ref, g_ref, win_ref, wc_ref, bc_ref, dtb_ref, alog_ref,
                     expand_ref, z_ref, xs_ref, xdt_ref, bt_ref, c_ref, cs_ref, cst_ref,
                     xcat_ref, hist_ref, bact_ref, dte_ref, *, proj_chunk, row_chunk, col_chunk):
    i = pl.program_id(1)
    tm = h_ref.shape[0]
    di = xs_ref.shape[1]
    gn = c_ref.shape[1]
    L = SSM_CHUNK
    halo = SUBLANES
    n_xbc = di + 2 * gn
    u = _rms(h_ref[...], g_ref[...]).astype(BF16)
    dt = jax.nn.softplus(_dot(u, win_ref[:, di + n_xbc:]) + dtb_ref[...])

    @pl.when(i == 0)
    def _():
        hist_ref[...] = jnp.zeros(hist_ref.shape, F32)

    def project_z(q):
        z_ref[:, q * proj_chunk:(q + 1) * proj_chunk] = _dot(u, win_ref[:, q * proj_chunk:(q + 1) * proj_chunk])

    def project_xbc(panel, slot):
        p0 = di + panel * proj_chunk
        xcat_ref[slot, 0:halo, :] = hist_ref[panel]
        xcat_ref[slot, halo:, :] = _dot(u, win_ref[:, p0:p0 + proj_chunk])

    def time_sums():
        r_i = lax.broadcasted_iota(jnp.int32, (L, L), 0)
        c_i = lax.broadcasted_iota(jnp.int32, (L, L), 1)
        tri = (r_i >= c_i).astype(F32)
        a_row = -jnp.exp(alog_ref[...]) * LOG2_E
        for c in range(tm // L):
            cs = jnp.dot(tri, dt[c * L:(c + 1) * L, :] * a_row, precision=lax.Precision.HIGHEST,
                         preferred_element_type=F32)
            cs_ref[c * L:(c + 1) * L, :] = cs
            cst_ref[c] = cs.T

    dt_hi = dt.astype(BF16)
    dt_lo = (dt - dt_hi.astype(F32)).astype(BF16)
    dt_terms = jnp.concatenate([dt_hi, dt_lo], axis=1)

    def expand_dt(panel):
        p0 = panel * proj_chunk
        dte_ref[...] = _dot(dt_terms, expand_ref[:, p0:p0 + proj_chunk])

    def conv_silu(panel, slot):
        p0 = panel * proj_chunk
        hist_ref[panel] = xcat_ref[slot, tm:tm + halo, :]
        for r0 in range(0, tm, row_chunk):
            for c0 in range(p0, p0 + proj_chunk, col_chunk):
                rows = slice(r0, r0 + row_chunk)
                cols = slice(c0, c0 + col_chunk)
                xb = xcat_ref[slot, r0:r0 + halo + row_chunk, c0 - p0:c0 - p0 + col_chunk]
                x1 = pltpu.roll(xb, 1, axis=0)
                near = wc_ref[3:4, cols] * xb + wc_ref[2:3, cols] * x1
                far = wc_ref[1:2, cols] * xb + wc_ref[0:1, cols] * x1
                acc = near[halo:, :] + pltpu.roll(far, 2, axis=0)[halo:, :] + bc_ref[:, cols]
                act = acc * jax.nn.sigmoid(acc)
                if c0 < di:
                    xs_ref[rows, cols] = act
                    xdt_ref[rows, cols] = (act * dte_ref[rows, c0 - p0:c0 - p0 + col_chunk]).astype(BF16)
                elif c0 < di + gn:
                    bact_ref[rows, c0 - di:c0 - di + col_chunk] = act
                else:
                    c_ref[rows, c0 - di - gn:c0 - di - gn + col_chunk] = act.astype(BF16)

    def transpose_b():
        for c in range(tm // L):
            for g in range(SSM_GROUPS):
                n0 = g * SSM_STATE
                bt_ref[c, n0:n0 + SSM_STATE, :] = bact_ref[c * L:(c + 1) * L, n0:n0 + SSM_STATE].T.astype(BF16)

    n_x, n_b = di // proj_chunk, gn // proj_chunk
    order = list(range(n_x, n_xbc // proj_chunk)) + list(range(n_x))
    slots = xcat_ref.shape[0]
    for step in range(slots - 1):
        project_xbc(order[step], step)
    for step, panel in enumerate(order):
        ahead = step + slots - 1
        if ahead < len(order):
            project_xbc(order[ahead], ahead % slots)
        if step < n_x:
            project_z(step)
        if step == 0:
            time_sums()
        if panel < n_x:
            expand_dt(panel)
        conv_silu(panel, step % slots)
        if panel == n_x + n_b - 1:
            transpose_b()


def _ssd_kernel(h_ref, z_ref, xs_ref, xdt_ref, bt_ref, c_ref, cs_ref, cst_ref, dexp_ref, ng_ref,
                wout_ref, o_ref, state_ref, y_ref, yn_ref):
    i = pl.program_id(1)
    tm = h_ref.shape[0]
    L = SSM_CHUNK
    gw = 4 * SSM_HEAD_DIM

    @pl.when(i == 0)
    def _():
        state_ref[...] = jnp.zeros(state_ref.shape, F32)

    r_i = lax.broadcasted_iota(jnp.int32, (L, L), 0)
    c_i = lax.broadcasted_iota(jnp.int32, (L, L), 1)
    causal = r_i >= c_i
    left = _left_half()

    def chunk(c, carry):
        r0 = pl.multiple_of(c * L, L)
        cs = cs_ref[pl.ds(r0, L), :]
        cs_t = cst_ref[c]
        cbs, y_offs = [], []
        for g in range(SSM_GROUPS):
            n0 = g * SSM_STATE
            cg = c_ref[pl.ds(r0, L), n0:n0 + SSM_STATE]
            cbs.append(_dot(cg, bt_ref[c, n0:n0 + SSM_STATE, :]))
            y_offs.append(_dot(cg, state_ref[g].astype(BF16)))
        xws, carry_decay = [], []
        for g in range(SSM_GROUPS):
            cb = cbs[g]
            xg = xdt_ref[pl.ds(r0, L), g * gw:(g + 1) * gw]
            y_tiles, ecs_tiles, dec_tiles = [], [], []
            for pair in range(2):
                xt = xg[:, pair * LANES:(pair + 1) * LANES]
                ms, bcs = [], []
                for half in range(2):
                    hh = 4 * g + 2 * pair + half
                    bc = jnp.broadcast_to(cs[:, hh:hh + 1], (L, L))
                    lm = jnp.exp2(jnp.where(causal, bc - cs_t[hh:hh + 1, :], -jnp.inf))
                    ms.append((cb * lm).astype(BF16))
                    bcs.append(bc)
                prod = _dot(jnp.concatenate(ms, axis=0), xt)
                y_tiles.append(jnp.where(left, prod[:L, :], prod[L:, :]))
                cs_e = jnp.where(left, bcs[0], bcs[1])
                ecs_tiles.append(jnp.exp2(cs_e))
                dec_tiles.append(jnp.exp2(cs_e[L - 1:L, :] - cs_e))
            ecs_e = jnp.concatenate(ecs_tiles, axis=1)
            y_off = y_offs[g] * ecs_e
            xws.append((xg.astype(F32) * jnp.concatenate(dec_tiles, axis=1)).astype(BF16))
            carry_decay.append(ecs_e[L - 1:L, :])
            cols = slice(g * gw, (g + 1) * gw)
            zz = z_ref[pl.ds(r0, L), cols]
            yg = jnp.concatenate(y_tiles, axis=1) + y_off + xs_ref[pl.ds(r0, L), cols] * dexp_ref[:, cols]
            yg = yg * (zz * jax.nn.sigmoid(zz))
            y_ref[pl.ds(r0, L), cols] = yg
            sq = yg * yg
            sq = sq[:, :LANES] + sq[:, LANES:]
            ssq = sq if g == 0 else ssq + sq
        for g in range(SSM_GROUPS):
            n0 = g * SSM_STATE
            state_ref[g] = state_ref[g] * carry_decay[g] + _dot(bt_ref[c, n0:n0 + SSM_STATE, :], xws[g])
        inv = lax.rsqrt(jnp.sum(ssq, axis=-1, keepdims=True) * (1.0 / y_ref.shape[1]) + EPS)
        yn_ref[pl.ds(r0, L), :] = (y_ref[pl.ds(r0, L), :] * inv * ng_ref[...]).astype(BF16)
        return carry

    lax.fori_loop(0, tm // L, chunk, 0)
    o_ref[...] = h_ref[...] + _dot(yn_ref[...], wout_ref[...])


def _mamba_layer(h, batch, seq, g, win, wc, bc, dtb, alog, dexp, ng, wout, *,
                 tm=512, proj_chunk=512):
    t, d = h.shape
    di = wout.shape[0]
    gn = SSM_GROUPS * SSM_STATE
    L = SSM_CHUNK
    tm = min(tm, seq)
    nt = seq // tm
    cur = lambda b, i: (b * nt + i, 0)
    expand = jnp.tile(jnp.repeat(jnp.eye(LANES, di // SSM_HEAD_DIM, dtype=BF16), SSM_HEAD_DIM, axis=1),
                      (2, 1))
    def fused(h_ref, g_ref, win_ref, wc_ref, bc_ref, dtb_ref, alog_ref, expand_ref, dexp_ref, ng_ref,
              wout_ref, o_ref, z_ref, xs_ref, xdt_ref, bt_ref, c_ref, cs_ref, cst_ref,
              xcat_ref, hist_ref, dte_ref, state_ref, y_ref, yn_ref):
        bact_ref = y_ref.at[:, 0:gn]
        _mamba_in_kernel(h_ref, g_ref, win_ref, wc_ref, bc_ref, dtb_ref, alog_ref, expand_ref,
                         z_ref, xs_ref, xdt_ref, bt_ref, c_ref, cs_ref, cst_ref,
                         xcat_ref, hist_ref, bact_ref, dte_ref,
                         proj_chunk=proj_chunk, row_chunk=128, col_chunk=128)
        _ssd_kernel(h_ref, z_ref, xs_ref, xdt_ref, bt_ref, c_ref, cs_ref, cst_ref, dexp_ref, ng_ref,
                    wout_ref, o_ref, state_ref, y_ref, yn_ref)

    return pl.pallas_call(
        fused,
        grid=(batch, nt),
        in_specs=[pl.BlockSpec((tm, d), cur), _const_spec(g.shape), _const_spec(win.shape),
                  _const_spec(wc.shape), _const_spec(bc.shape), _const_spec(dtb.shape),
                  _const_spec(alog.shape), _const_spec(expand.shape), _const_spec(dexp.shape),
                  _const_spec(ng.shape), _const_spec(wout.shape)],
        out_specs=pl.BlockSpec((tm, d), cur),
        out_shape=jax.ShapeDtypeStruct((t, d), F32),
        scratch_shapes=[
            pltpu.VMEM((tm, di), F32), pltpu.VMEM((tm, di), F32), pltpu.VMEM((tm, di), BF16),
            pltpu.VMEM((tm // L, gn, L), BF16), pltpu.VMEM((tm, gn), BF16),
            pltpu.VMEM((tm, LANES), F32), pltpu.VMEM((tm // L, LANES, L), F32),
            pltpu.VMEM((2, tm + SUBLANES, proj_chunk), F32),
            pltpu.VMEM(((di + 2 * gn) // proj_chunk, SUBLANES, proj_chunk), F32),
            pltpu.VMEM((tm, proj_chunk), F32),
            pltpu.VMEM((SSM_GROUPS, SSM_STATE, 4 * SSM_HEAD_DIM), F32),
            pltpu.VMEM((tm, di), F32), pltpu.VMEM((tm, di), BF16)],
        compiler_params=_params(2),
        name="mamba",
    )(h, g, win, wc, bc, dtb, alog, expand, dexp, ng, wout)


def _row(v):
    return v.reshape(1, -1).astype(F32)


def _pad_cols(a, n):
    return jnp.pad(a, ((0, 0), (0, n - a.shape[1])))


def kernel(x, p, mix_norm_g, mlp_norm_g, ple_norm_g, a_wqkv, a_q_norm_g, a_k_norm_g, a_sinks, a_wo, b_w_pw1, b_b_pw1, b_w_dw, b_b_dw, b_ln_g, b_ln_b, b_w_pw2, b_b_pw2, c_w_in, c_w_conv, c_b_conv, c_dt_bias, c_A_log, c_D, c_norm_g, c_w_out, m_w1, m_w2, ple_w_proj, ple_w_gate):
    batch, seq, d = x.shape
    depth = p.shape[0]
    t = batch * seq
    h = x.reshape(t, d)
    p = p.reshape(depth, t, -1)
    w1_all, w2_all = m_w1.astype(BF16), m_w2.astype(BF16)
    wg_all, wp_all = ple_w_gate.astype(BF16), ple_w_proj.astype(BF16)
    for i in range(depth):
        kind, j = i % N_MIXERS, i // N_MIXERS
        g = _row(mix_norm_g[i])
        if kind == 0:
            scale = ATTN_HEAD_DIM ** -0.5
            gq = _row(jnp.tile(a_q_norm_g[j] * scale, 2))
            gk = _row(jnp.tile(a_k_norm_g[j], 2))
            h = _attention_layer(h, batch, seq, g, a_wqkv[j].astype(BF16), gq, gk,
                                 a_sinks[j].astype(F32), a_wo[j].astype(BF16))
        elif kind == 1:
            wdw = jnp.pad(b_w_dw[j], ((0, CONV_HALO - CONV_WIDTH), (0, 0)))
            h = _conformer_layer(h, batch, seq, g, b_w_pw1[j].astype(BF16), _row(b_b_pw1[j]),
                                 wdw, _row(b_b_dw[j]), _row(b_ln_g[j]), _row(b_ln_b[j]),
                                 b_w_pw2[j].astype(BF16), _row(b_b_pw2[j]))
        else:
            di = c_w_out.shape[1]
            gn = SSM_GROUPS * SSM_STATE
            win = _pad_cols(c_w_in[j], 2 * di + 2 * gn + LANES).astype(BF16)
            wc = jnp.pad(c_w_conv[j], ((0, SUBLANES - SSM_CONV), (0, 0)))
            dtb = _pad_cols(_row(c_dt_bias[j]), LANES)
            alog = _pad_cols(_row(c_A_log[j]), LANES)
            dexp = _row(jnp.repeat(c_D[j], SSM_HEAD_DIM))
            h = _mamba_layer(h, batch, seq, g, win, wc, _row(c_b_conv[j]), dtb, alog,
                             dexp, _row(c_norm_g[j]), c_w_out[j].astype(BF16))
        h = _mlp_ple(h, p, i, _row(mlp_norm_g[i]), _row(ple_norm_g[i]), w1_all, w2_all, wg_all, wp_all)
    return h.reshape(batch, seq, d)
```

```python
import functools

import jax
import jax.numpy as jnp
from jax import lax
from jax.experimental import pallas as pl
from jax.experimental.pallas import tpu as pltpu

F32 = jnp.float32
BF16 = jnp.bfloat16
EPS = 1e-6
LOG2_E = 1.4426950408889634

N_MIXERS = 3
ATTN_HEADS = 16
ATTN_KV_HEADS = 4
ATTN_HEAD_DIM = 64
ATTN_BLOCK = 128
CONV_WIDTH = 31
SSM_HEAD_DIM = 64
SSM_GROUPS = 8
SSM_STATE = 128
SSM_CONV = 4
SSM_CHUNK = 128

LANES = 128
SUBLANES = 8
VMEM_LIMIT_BYTES = 56 * 1024 * 1024


def _params(n_grid_dims):
    return pltpu.CompilerParams(
        dimension_semantics=("arbitrary",) * n_grid_dims,
        vmem_limit_bytes=VMEM_LIMIT_BYTES)


def _const_spec(shape):
    nd = len(shape)
    return pl.BlockSpec(shape, lambda *_: (0,) * nd, pipeline_mode=pl.Buffered(1))


def _rms(x, g):
    return x * lax.rsqrt(jnp.mean(x * x, axis=-1, keepdims=True) + EPS) * g


def _dot(a, b):
    return jnp.dot(a, b, preferred_element_type=F32)


def _dot_nt(a, b):
    return lax.dot_general(a, b, (((1,), (1,)), ((), ())), preferred_element_type=F32)


def _left_half():
    return lax.broadcasted_iota(jnp.int32, (1, LANES), 1) < (LANES // 2)


def _mlp_ple_kernel(h_ref, p_ref, g1_ref, g2_ref, w1_ref, w2_ref, wg_ref, wp_ref, o_ref, *,
                    hidden_chunk):
    x = h_ref[...]
    u = _rms(x, g1_ref[...]).astype(BF16)
    acc = jnp.zeros_like(x)
    for c in range(0, w1_ref.shape[1], hidden_chunk):
        a = _dot(u, w1_ref[:, c:c + hidden_chunk])
        a = jnp.square(jnp.maximum(a, 0.0)).astype(BF16)
        acc = acc + _dot(a, w2_ref[c:c + hidden_chunk, :])
    h2 = x + acc
    u2 = _rms(h2, g2_ref[...]).astype(BF16)
    gate = jax.nn.sigmoid(_dot(u2, wg_ref[...]))
    proj = _dot(p_ref[...].astype(BF16), wp_ref[...])
    o_ref[...] = h2 + gate * proj


def _layer_spec(stack, layer):
    return pl.BlockSpec((None,) + stack.shape[1:], lambda *_: (layer, 0, 0), pipeline_mode=pl.Buffered(1))


def _mlp_ple(h, p, layer, g1, g2, w1, w2, wg, wp, *, tm=1024, hidden_chunk=512):
    t, d = h.shape
    tm = min(tm, t)
    row = lambda i: (i, 0)
    return pl.pallas_call(
        functools.partial(_mlp_ple_kernel, hidden_chunk=hidden_chunk),
        grid=(t // tm,),
        in_specs=[pl.BlockSpec((tm, d), row),
                  pl.BlockSpec((None, tm, p.shape[2]), lambda i: (layer, i, 0)),
                  _const_spec(g1.shape), _const_spec(g2.shape), _layer_spec(w1, layer),
                  _layer_spec(w2, layer), _layer_spec(wg, layer), _layer_spec(wp, layer)],
        out_specs=pl.BlockSpec((tm, d), row),
        out_shape=jax.ShapeDtypeStruct((t, d), F32),
        compiler_params=_params(1),
        name="mlp_ple",
    )(h, p, g1, g2, w1, w2, wg, wp)


def _qkv_kernel(h_ref, g_ref, w_ref, gq_ref, gk_ref, seg_ref, q_ref, ke_ref, ko_ref, v2_ref, *, panel):
    hd = ATTN_HEAD_DIM
    nq = q_ref.shape[1]
    nk = ATTN_KV_HEADS * hd
    u = _rms(h_ref[...], g_ref[...]).astype(BF16)
    left = _left_half()

    def pair_norm(t, g):
        sq = t * t
        sq_hi = sq.astype(BF16)
        sq_lo = (sq - sq_hi.astype(F32)).astype(BF16)
        ssq = _dot(jnp.concatenate([sq_hi, sq_lo], axis=1), seg_ref[...])
        return t * lax.rsqrt(ssq * (1.0 / hd) + EPS) * g

    def project(c0):
        return _dot(u, w_ref[:, c0:c0 + panel])

    def finish(c0, acc):
        for t0 in range(0, panel, LANES):
            c = c0 + t0
            t = acc[:, t0:t0 + LANES]
            if c < nq:
                q_ref[:, c:c + LANES] = pair_norm(t, gq_ref[...]).astype(BF16)
                continue
            j = (c - nq) % nk // LANES
            first, second = 2 * j * LANES, (2 * j + 1) * LANES
            if c < nq + nk:
                t = pair_norm(t, gk_ref[...])
                r = pltpu.roll(t, hd, axis=1)
                zero = jnp.zeros_like(t)
                ke_ref[:, first:first + LANES] = jnp.where(left, t, zero).astype(BF16)
                ke_ref[:, second:second + LANES] = jnp.where(left, r, zero).astype(BF16)
                ko_ref[:, first:first + LANES] = jnp.where(left, zero, r).astype(BF16)
                ko_ref[:, second:second + LANES] = jnp.where(left, zero, t).astype(BF16)
            else:
                r = pltpu.roll(t, hd, axis=1)
                v2_ref[:, first:first + LANES] = jnp.where(left, t, r).astype(BF16)
                v2_ref[:, second:second + LANES] = jnp.where(left, r, t).astype(BF16)

    starts = list(range(0, w_ref.shape[1], panel))
    acc = project(starts[0])
    for idx, c0 in enumerate(starts):
        nxt = project(starts[idx + 1]) if idx + 1 < len(starts) else None
        finish(c0, acc)
        acc = nxt


def _attn_core(sink_ref, bias_ref, h_ref, q_ref, ke_ref, ko_ref, v_ref, wo_ref, o_ref, oacc_ref):
    blk = ATTN_BLOCK
    group = ATTN_HEADS // ATTN_KV_HEADS
    i = pl.program_id(1)
    tq = q_ref.shape[0]

    row = lax.broadcasted_iota(jnp.int32, (blk, blk), 0)
    col = lax.broadcasted_iota(jnp.int32, (blk, blk), 1)
    lower = col <= row
    no_prev = (col - row) > jnp.where(i > 0, blk, 0)
    left = _left_half()

    def scores(b0, kv):
        lanes = slice(kv * LANES, (kv + 1) * LANES)
        qt = jnp.concatenate(
            [q_ref[b0:b0 + blk, (2 * kv) * LANES:(2 * kv + 1) * LANES],
             q_ref[b0:b0 + blk, (2 * kv + 1) * LANES:(2 * kv + 2) * LANES]], axis=0)
        return (_dot_nt(qt, ke_ref[b0:b0 + 2 * blk, lanes]),
                _dot_nt(qt, ko_ref[b0:b0 + 2 * blk, lanes]))

    items = [(b0, kv) for b0 in range(0, tq, blk) for kv in range(ATTN_KV_HEADS)]
    ahead = scores(*items[0])
    for idx, (b0, kv) in enumerate(items):
        lanes = slice(kv * LANES, (kv + 1) * LANES)
        s_even, s_odd = ahead
        if idx + 1 < len(items):
            ahead = scores(*items[idx + 1])
        probs, r_den = [], []
        for g in range(group):
            src = s_even if g % 2 == 0 else s_odd
            r0 = (g // 2) * blk
            hh = kv * group + g
            s = jnp.where(lower, src[r0:r0 + blk, blk:], src[r0:r0 + blk, :blk]) + bias_ref[hh]
            if b0 == 0:
                s = jnp.where(no_prev, -jnp.inf, s)
            sink = sink_ref[hh]
            m = jnp.maximum(jnp.max(s, axis=-1, keepdims=True), sink)
            e = jnp.exp(s - m)
            r_den.append(1.0 / (jnp.sum(e, axis=-1, keepdims=True) + jnp.exp(sink - m)))
            probs.append(jnp.concatenate([jnp.where(lower, 0.0, e), jnp.where(lower, e, 0.0)],
                                         axis=1).astype(BF16))
        o4 = _dot(jnp.concatenate(probs, axis=0), v_ref[b0:b0 + 2 * blk, lanes])
        for pair in range(group // 2):
            ga, gb = 2 * pair, 2 * pair + 1
            oa = o4[ga * blk:(ga + 1) * blk, :] * r_den[ga]
            ob = o4[gb * blk:(gb + 1) * blk, :] * r_den[gb]
            c0 = (2 * kv + pair) * LANES
            oacc_ref[b0:b0 + blk, c0:c0 + LANES] = jnp.where(left, oa, ob).astype(BF16)
    o_ref[...] = h_ref[...] + _dot(oacc_ref[...], wo_ref[...])


def _alibi_bias():
    blk = ATTN_BLOCK
    row = jnp.arange(blk, dtype=jnp.int32)[:, None]
    col = jnp.arange(blk, dtype=jnp.int32)[None, :]
    rel = jnp.where(col <= row, row - col, row - col + blk).astype(F32)
    slopes = jnp.exp2(-8.0 * (jnp.arange(ATTN_HEADS, dtype=F32) + 1.0) / ATTN_HEADS)
    return -(slopes[:, None, None] * rel[None])


def _attention_layer(h, batch, seq, g, wqkv, gq, gk, sinks, wo, *, tm=1024):
    t, d = h.shape
    hd, blk = ATTN_HEAD_DIM, ATTN_BLOCK
    nq, nkp = ATTN_HEADS * hd, ATTN_KV_HEADS * LANES
    tm = min(tm, seq)
    lane_head = jnp.arange(LANES) // hd
    seg = jnp.tile((lane_head[:, None] == lane_head[None, :]).astype(BF16), (2, 1))
    bias = _alibi_bias()
    nt = seq // tm
    cur = lambda b, i: (b * nt + i, 0)

    def fused(sink_ref, bias_ref, h_ref, g_ref, wqkv_ref, gq_ref, gk_ref, seg_ref, wo_ref, o_ref,
              q_ref, ke_ref, ko_ref, v_ref, oacc_ref):
        i = pl.program_id(1)
        kv_bufs = (ke_ref, ko_ref, v_ref)

        @pl.when(i == 0)
        def _():
            for buf in kv_bufs:
                buf[0:blk, :] = jnp.zeros((blk, nkp), BF16)

        _qkv_kernel(h_ref, g_ref, wqkv_ref, gq_ref, gk_ref, seg_ref, q_ref,
                    *(buf.at[blk:, :] for buf in kv_bufs), panel=2 * LANES)
        _attn_core(sink_ref, bias_ref, h_ref, q_ref, ke_ref, ko_ref, v_ref, wo_ref, o_ref, oacc_ref)
        for buf in kv_bufs:
            buf[0:blk, :] = buf[tm:tm + blk, :]

    return pl.pallas_call(
        fused,
        grid=(batch, nt),
        in_specs=[pl.BlockSpec(memory_space=pltpu.SMEM), _const_spec(bias.shape),
                  pl.BlockSpec((tm, d), cur), _const_spec(g.shape), _const_spec(wqkv.shape),
                  _const_spec(gq.shape), _const_spec(gk.shape), _const_spec(seg.shape),
                  _const_spec(wo.shape)],
        out_specs=pl.BlockSpec((tm, d), cur),
        out_shape=jax.ShapeDtypeStruct((t, d), F32),
        scratch_shapes=[pltpu.VMEM((tm, nq), BF16), pltpu.VMEM((tm + blk, nkp), BF16),
                        pltpu.VMEM((tm + blk, nkp), BF16), pltpu.VMEM((tm + blk, nkp), BF16),
                        pltpu.VMEM((tm, nq), BF16)],
        compiler_params=_params(2),
        name="attention",
    )(sinks, bias, h, g, wqkv, gq, gk, seg, wo)


CONV_HALO = 32


def _conformer_kernel(h_ref, g_ref, w1_ref, b1_ref, wdw_ref, bdw_ref, lg_ref, lb_ref, w2_ref, b2_ref,
                      o_ref, ycat_ref, ysh_ref, conv_ref, *, row_chunk, col_chunk):
    i = pl.program_id(1)
    n_streams, tm, d = h_ref.shape
    halo = CONV_HALO
    span = tm + halo - SUBLANES
    first = halo - (CONV_WIDTH - 1)

    @pl.when(i == 0)
    def _():
        ycat_ref[:, 0:halo, :] = jnp.zeros((n_streams, halo, d), F32)

    for s in range(n_streams):
        u = _rms(h_ref[s], g_ref[...]).astype(BF16)
        a = _dot(u, w1_ref[...]) + b1_ref[...]
        ycat_ref[s, halo:, :] = a[:, :d] * jax.nn.sigmoid(a[:, d:])

    for s in range(n_streams):
        for ph in range(1, SUBLANES):
            ysh_ref[s, ph - 1] = ycat_ref[s, ph:ph + span, :]
        for r0 in range(0, tm, row_chunk):
            for c0 in range(0, d, col_chunk):
                cols = slice(c0, c0 + col_chunk)
                acc = jnp.zeros((row_chunk, col_chunk), F32)
                for k in range(CONV_WIDTH):
                    phase = (first + k) % SUBLANES
                    base = r0 + first + k - phase
                    if phase == 0:
                        tap = ycat_ref[s, base:base + row_chunk, cols]
                    else:
                        tap = ysh_ref[s, phase - 1, base:base + row_chunk, cols]
                    acc = acc + wdw_ref[k:k + 1, cols] * tap
                conv_ref[s, r0:r0 + row_chunk, cols] = acc + bdw_ref[:, cols]
        ycat_ref[s, 0:halo, :] = ycat_ref[s, tm:tm + halo, :]

    for s in range(n_streams):
        c = conv_ref[s]
        mu = jnp.mean(c, axis=-1, keepdims=True)
        cc = c - mu
        var = jnp.mean(cc * cc, axis=-1, keepdims=True)
        y = cc * lax.rsqrt(var + EPS) * lg_ref[...] + lb_ref[...]
        y = y * jax.nn.sigmoid(y)
        o_ref[s] = h_ref[s] + _dot(y.astype(BF16), w2_ref[...]) + b2_ref[...]


def _conformer_layer(h, batch, seq, g, w1, b1, wdw, bdw, lg, lb, w2, b2, *, tm=256, n_streams=2):
    t, d = h.shape
    tm = min(tm, seq)
    n_streams = min(n_streams, batch)
    cur = lambda b, i: (b, i, 0)
    out = pl.pallas_call(
        functools.partial(_conformer_kernel, row_chunk=128, col_chunk=128),
        grid=(batch // n_streams, seq // tm),
        in_specs=[pl.BlockSpec((n_streams, tm, d), cur), _const_spec(g.shape), _const_spec(w1.shape),
                  _const_spec(b1.shape), _const_spec(wdw.shape), _const_spec(bdw.shape),
                  _const_spec(lg.shape), _const_spec(lb.shape), _const_spec(w2.shape),
                  _const_spec(b2.shape)],
        out_specs=pl.BlockSpec((n_streams, tm, d), cur),
        out_shape=jax.ShapeDtypeStruct((batch, seq, d), F32),
        scratch_shapes=[pltpu.VMEM((n_streams, tm + CONV_HALO, d), F32),
                        pltpu.VMEM((n_streams, SUBLANES - 1, tm + CONV_HALO - SUBLANES, d), F32),
                        pltpu.VMEM((n_streams, tm, d), F32)],
        compiler_params=_params(2),
        name="conformer",
    )(h.reshape(batch, seq, d), g, w1, b1, wdw, bdw, lg, lb, w2, b2)
    return out.reshape(t, d)


def _mamba_in_kernel(h_ref, g_ref, win_ref, wc_ref, bc_ref, dtb_ref, alog_ref,
                     expand_ref, z_ref, xs_ref, xdt_ref, bt_ref, c_ref, cs_ref, cst_ref,
                     xcat_ref, hist_ref, bact_ref, dte_ref, *, proj_chunk, row_chunk, col_chunk):
    i = pl.program_id(1)
    tm = h_ref.shape[0]
    di = xs_ref.shape[1]
    gn = c_ref.shape[1]
    L = SSM_CHUNK
    halo = SUBLANES
    n_xbc = di + 2 * gn
    u = _rms(h_ref[...], g_ref[...]).astype(BF16)
    dt = jax.nn.softplus(_dot(u, win_ref[:, di + n_xbc:]) + dtb_ref[...])

    @pl.when(i == 0)
    def _():
        hist_ref[...] = jnp.zeros(hist_ref.shape, F32)

    def project_z(q):
        z_ref[:, q * proj_chunk:(q + 1) * proj_chunk] = _dot(u, win_ref[:, q * proj_chunk:(q + 1) * proj_chunk])

    def project_xbc(panel, slot):
        p0 = di + panel * proj_chunk
        xcat_ref[slot, 0:halo, :] = hist_ref[panel]
        xcat_ref[slot, halo:, :] = _dot(u, win_ref[:, p0:p0 + proj_chunk])

    def time_sums():
        r_i = lax.broadcasted_iota(jnp.int32, (L, L), 0)
        c_i = lax.broadcasted_iota(jnp.int32, (L, L), 1)
        tri = (r_i >= c_i).astype(BF16)
        da = dt * (-jnp.exp(alog_ref[...]) * LOG2_E)
        hi = da.astype(BF16)
        rem = da - hi.astype(F32)
        mid = rem.astype(BF16)
        lo = (rem - mid.astype(F32)).astype(BF16)
        n_chunks = tm // L
        terms = jnp.concatenate([t[c * L:(c + 1) * L, :] for c in range(n_chunks) for t in (hi, mid, lo)],
                                axis=1)
        sums = _dot(tri, terms)
        for c in range(n_chunks):
            s0 = 3 * LANES * c
            cs = sums[:, s0:s0 + LANES] + sums[:, s0 + LANES:s0 + 2 * LANES] + sums[:, s0 + 2 * LANES:s0 + 3 * LANES]
            cs_ref[c * L:(c + 1) * L, :] = cs
            cst_ref[c] = cs.T

    dt_hi = dt.astype(BF16)
    dt_lo = (dt - dt_hi.astype(F32)).astype(BF16)
    dt_terms = jnp.concatenate([dt_hi, dt_lo], axis=1)

    def expand_dt(panel):
        p0 = panel * proj_chunk
        dte_ref[...] = _dot(dt_terms, expand_ref[:, p0:p0 + proj_chunk])

    def conv_silu(panel, slot):
        p0 = panel * proj_chunk
        hist_ref[panel] = xcat_ref[slot, tm:tm + halo, :]
        for r0 in range(0, tm, row_chunk):
            for c0 in range(p0, p0 + proj_chunk, col_chunk):
                rows = slice(r0, r0 + row_chunk)
                cols = slice(c0, c0 + col_chunk)
                xb = xcat_ref[slot, r0:r0 + halo + row_chunk, c0 - p0:c0 - p0 + col_chunk]
                x1 = pltpu.roll(xb, 1, axis=0)
                near = wc_ref[3:4, cols] * xb + wc_ref[2:3, cols] * x1
                far = wc_ref[1:2, cols] * xb + wc_ref[0:1, cols] * x1
                acc = near[halo:, :] + pltpu.roll(far, 2, axis=0)[halo:, :] + bc_ref[:, cols]
                act = acc * jax.nn.sigmoid(acc)
                if c0 < di:
                    xs_ref[rows, cols] = act
                    xdt_ref[rows, cols] = (act * dte_ref[rows, c0 - p0:c0 - p0 + col_chunk]).astype(BF16)
                elif c0 < di + gn:
                    bact_ref[rows, c0 - di:c0 - di + col_chunk] = act
                else:
                    c_ref[rows, c0 - di - gn:c0 - di - gn + col_chunk] = act.astype(BF16)

    def transpose_b():
        for c in range(tm // L):
            for g in range(SSM_GROUPS):
                n0 = g * SSM_STATE
                bt_ref[c, n0:n0 + SSM_STATE, :] = bact_ref[c * L:(c + 1) * L, n0:n0 + SSM_STATE].T.astype(BF16)

    n_x, n_b = di // proj_chunk, gn // proj_chunk
    order = list(range(n_x, n_xbc // proj_chunk)) + list(range(n_x))
    slots = xcat_ref.shape[0]
    for step in range(slots - 1):
        project_xbc(order[step], step)
    for step, panel in enumerate(order):
        ahead = step + slots - 1
        if ahead < len(order):
            project_xbc(order[ahead], ahead % slots)
        if step < n_x:
            project_z(step)
        if step == 0:
            time_sums()
        if panel < n_x:
            expand_dt(panel)
        conv_silu(panel, step % slots)
        if panel == n_x + n_b - 1:
            transpose_b()


def _ssd_kernel(h_ref, z_ref, xs_ref, xdt_ref, bt_ref, c_ref, cs_ref, cst_ref, dexp_ref, ng_ref,
                wout_ref, o_ref, state_ref, y_ref, yn_ref):
    i = pl.program_id(1)
    tm = h_ref.shape[0]
    L = SSM_CHUNK
    gw = 4 * SSM_HEAD_DIM

    @pl.when(i == 0)
    def _():
        state_ref[...] = jnp.zeros(state_ref.shape, F32)

    r_i = lax.broadcasted_iota(jnp.int32, (L, L), 0)
    c_i = lax.broadcasted_iota(jnp.int32, (L, L), 1)
    causal = r_i >= c_i
    left = _left_half()

    def chunk(c, carry):
        r0 = pl.multiple_of(c * L, L)
        cs = cs_ref[pl.ds(r0, L), :]
        cs_t = cst_ref[c]
        cbs, y_offs = [], []
        for g in range(SSM_GROUPS):
            n0 = g * SSM_STATE
            cg = c_ref[pl.ds(r0, L), n0:n0 + SSM_STATE]
            cbs.append(_dot(cg, bt_ref[c, n0:n0 + SSM_STATE, :]))
            y_offs.append(_dot(cg, state_ref[g].astype(BF16)))
        xws, carry_decay = [], []
        for g in range(SSM_GROUPS):
            cb = cbs[g]
            xg = xdt_ref[pl.ds(r0, L), g * gw:(g + 1) * gw]
            y_tiles, ecs_tiles, dec_tiles = [], [], []
            for pair in range(2):
                xt = xg[:, pair * LANES:(pair + 1) * LANES]
                ms, bcs = [], []
                for half in range(2):
                    hh = 4 * g + 2 * pair + half
                    bc = jnp.broadcast_to(cs[:, hh:hh + 1], (L, L))
                    lm = jnp.exp2(jnp.where(causal, bc - cs_t[hh:hh + 1, :], -jnp.inf))
                    ms.append((cb * lm).astype(BF16))
                    bcs.append(bc)
                prod = _dot(jnp.concatenate(ms, axis=0), xt)
                y_tiles.append(jnp.where(left, prod[:L, :], prod[L:, :]))
                cs_e = jnp.where(left, bcs[0], bcs[1])
                ecs_tiles.append(jnp.exp2(cs_e))
                dec_tiles.append(jnp.exp2(cs_e[L - 1:L, :] - cs_e))
            ecs_e = jnp.concatenate(ecs_tiles, axis=1)
            y_off = y_offs[g] * ecs_e
            xws.append((xg.astype(F32) * jnp.concatenate(dec_tiles, axis=1)).astype(BF16))
            carry_decay.append(ecs_e[L - 1:L, :])
            cols = slice(g * gw, (g + 1) * gw)
            zz = z_ref[pl.ds(r0, L), cols]
            yg = jnp.concatenate(y_tiles, axis=1) + y_off + xs_ref[pl.ds(r0, L), cols] * dexp_ref[:, cols]
            yg = yg * (zz * jax.nn.sigmoid(zz))
            y_ref[pl.ds(r0, L), cols] = yg
            sq = yg * yg
            sq = sq[:, :LANES] + sq[:, LANES:]
            ssq = sq if g == 0 else ssq + sq
        for g in range(SSM_GROUPS):
            n0 = g * SSM_STATE
            state_ref[g] = state_ref[g] * carry_decay[g] + _dot(bt_ref[c, n0:n0 + SSM_STATE, :], xws[g])
        inv = lax.rsqrt(jnp.sum(ssq, axis=-1, keepdims=True) * (1.0 / y_ref.shape[1]) + EPS)
        yn_ref[pl.ds(r0, L), :] = (y_ref[pl.ds(r0, L), :] * inv * ng_ref[...]).astype(BF16)
        return carry

    lax.fori_loop(0, tm // L, chunk, 0)
    o_ref[...] = h_ref[...] + _dot(yn_ref[...], wout_ref[...])


def _mamba_layer(h, batch, seq, g, win, wc, bc, dtb, alog, dexp, ng, wout, *,
                 tm=512, proj_chunk=512):
    t, d = h.shape
    di = wout.shape[0]
    gn = SSM_GROUPS * SSM_STATE
    L = SSM_CHUNK
    tm = min(tm, seq)
    nt = seq // tm
    cur = lambda b, i: (b * nt + i, 0)
    expand = jnp.tile(jnp.repeat(jnp.eye(LANES, di // SSM_HEAD_DIM, dtype=BF16), SSM_HEAD_DIM, axis=1),
                      (2, 1))
    def fused(h_ref, g_ref, win_ref, wc_ref, bc_ref, dtb_ref, alog_ref, expand_ref, dexp_ref, ng_ref,
              wout_ref, o_ref, z_ref, xs_ref, xdt_ref, bt_ref, c_ref, cs_ref, cst_ref,
              xcat_ref, hist_ref, dte_ref, state_ref, y_ref, yn_ref):
        bact_ref = y_ref.at[:, 0:gn]
        _mamba_in_kernel(h_ref, g_ref, win_ref, wc_ref, bc_ref, dtb_ref, alog_ref, expand_ref,
                         z_ref, xs_ref, xdt_ref, bt_ref, c_ref, cs_ref, cst_ref,
                         xcat_ref, hist_ref, bact_ref, dte_ref,
                         proj_chunk=proj_chunk, row_chunk=128, col_chunk=128)
        _ssd_kernel(h_ref, z_ref, xs_ref, xdt_ref, bt_ref, c_ref, cs_ref, cst_ref, dexp_ref, ng_ref,
                    wout_ref, o_ref, state_ref, y_ref, yn_ref)

    return pl.pallas_call(
        fused,
        grid=(batch, nt),
        in_specs=[pl.BlockSpec((tm, d), cur), _const_spec(g.shape), _const_spec(win.shape),
                  _const_spec(wc.shape), _const_spec(bc.shape), _const_spec(dtb.shape),
                  _const_spec(alog.shape), _const_spec(expand.shape), _const_spec(dexp.shape),
                  _const_spec(ng.shape), _const_spec(wout.shape)],
        out_specs=pl.BlockSpec((tm, d), cur),
        out_shape=jax.ShapeDtypeStruct((t, d), F32),
        scratch_shapes=[
            pltpu.VMEM((tm, di), F32), pltpu.VMEM((tm, di), F32), pltpu.VMEM((tm, di), BF16),
            pltpu.VMEM((tm // L, gn, L), BF16), pltpu.VMEM((tm, gn), BF16),
            pltpu.VMEM((tm, LANES), F32), pltpu.VMEM((tm // L, LANES, L), F32),
            pltpu.VMEM((2, tm + SUBLANES, proj_chunk), F32),
            pltpu.VMEM(((di + 2 * gn) // proj_chunk, SUBLANES, proj_chunk), F32),
            pltpu.VMEM((tm, proj_chunk), F32),
            pltpu.VMEM((SSM_GROUPS, SSM_STATE, 4 * SSM_HEAD_DIM), F32),
            pltpu.VMEM((tm, di), F32), pltpu.VMEM((tm, di), BF16)],
        compiler_params=_params(2),
        name="mamba",
    )(h, g, win, wc, bc, dtb, alog, expand, dexp, ng, wout)


def _row(v):
    return v.reshape(1, -1).astype(F32)


def _pad_cols(a, n):
    return jnp.pad(a, ((0, 0), (0, n - a.shape[1])))


def kernel(x, p, mix_norm_g, mlp_norm_g, ple_norm_g, a_wqkv, a_q_norm_g, a_k_norm_g, a_sinks, a_wo, b_w_pw1, b_b_pw1, b_w_dw, b_b_dw, b_ln_g, b_ln_b, b_w_pw2, b_b_pw2, c_w_in, c_w_conv, c_b_conv, c_dt_bias, c_A_log, c_D, c_norm_g, c_w_out, m_w1, m_w2, ple_w_proj, ple_w_gate):
    batch, seq, d = x.shape
    depth = p.shape[0]
    t = batch * seq
    h = x.reshape(t, d)
    p = p.reshape(depth, t, -1)
    w1_all, w2_all = m_w1.astype(BF16), m_w2.astype(BF16)
    wg_all, wp_all = ple_w_gate.astype(BF16), ple_w_proj.astype(BF16)
    for i in range(depth):
        kind, j = i % N_MIXERS, i // N_MIXERS
        g = _row(mix_norm_g[i])
        if kind == 0:
            scale = ATTN_HEAD_DIM ** -0.5
            gq = _row(jnp.tile(a_q_norm_g[j] * scale, 2))
            gk = _row(jnp.tile(a_k_norm_g[j], 2))
            h = _attention_layer(h, batch, seq, g, a_wqkv[j].astype(BF16), gq, gk,
                                 a_sinks[j].astype(F32), a_wo[j].astype(BF16))
        elif kind == 1:
            wdw = jnp.pad(b_w_dw[j], ((0, CONV_HALO - CONV_WIDTH), (0, 0)))
            h = _conformer_layer(h, batch, seq, g, b_w_pw1[j].astype(BF16), _row(b_b_pw1[j]),
                                 wdw, _row(b_b_dw[j]), _row(b_ln_g[j]), _row(b_ln_b[j]),
                                 b_w_pw2[j].astype(BF16), _row(b_b_pw2[j]))
        else:
            di = c_w_out.shape[1]
            gn = SSM_GROUPS * SSM_STATE
            win = _pad_cols(c_w_in[j], 2 * di + 2 * gn + LANES).astype(BF16)
            wc = jnp.pad(c_w_conv[j], ((0, SUBLANES - SSM_CONV), (0, 0)))
            dtb = _pad_cols(_row(c_dt_bias[j]), LANES)
            alog = _pad_cols(_row(c_A_log[j]), LANES)
            dexp = _row(jnp.repeat(c_D[j], SSM_HEAD_DIM))
            h = _mamba_layer(h, batch, seq, g, win, wc, _row(c_b_conv[j]), dtb, alog,
                             dexp, _row(c_norm_g[j]), c_w_out[j].astype(BF16))
        h = _mlp_ple(h, p, i, _row(mlp_norm_g[i]), _row(ple_norm_g[i]), w1_all, w2_all, wg_all, wp_all)
    return h.reshape(batch, seq, d)
```
